```python
import jax, jax.numpy as jnp
from jax import lax
import numpy as np

D_MODEL = 1024
BATCH = 8
SEQ = 4096
DEPTH = 1
DEC_BATCH = 16
DEC_SEQ = 64
PAST_LEN = 1024

CHUNK = 64
SB_HEADS = 8
SB_HEAD_DIM = 64
SB_WIDTH = SB_HEADS * SB_HEAD_DIM
SB_SCALE = SB_HEAD_DIM ** -0.5
Q_BLOCK = 128
POOL_WINDOWS = (2, 4, 8, 16)
POOL_GROUPS = len(POOL_WINDOWS)
POOL_WIDTH = D_MODEL // 2
POOL_GROUP_WIDTH = POOL_WIDTH // POOL_GROUPS
POOL_STATE = max(POOL_WINDOWS) - 1
IN_SPLITS = (SB_WIDTH, 2 * SB_WIDTH, 3 * SB_WIDTH, 3 * SB_WIDTH + POOL_WIDTH,
             3 * SB_WIDTH + POOL_WIDTH + D_MODEL)
IN_WIDTH = 3 * SB_WIDTH + POOL_WIDTH + 2 * D_MODEL
N_EXPERTS = 256
TOP_K = 8
N_EXPERT_GROUPS = 8
TOPK_GROUPS = 4
EXPERT_DIM = D_MODEL // 4
SHARED_DIM = D_MODEL // 4
ROUTED_SCALE = 2.5
MOE_BLOCK = 128
LN_EPS = 1e-5
DEEPNORM_ALPHA = (2 * DEPTH) ** 0.25
DEEPNORM_BETA = (8 * DEPTH) ** -0.25

kernel_name = 'stickbreak_pool_moe_streaming_step'


def layer_norm(x, g, b):
    xf = x.astype(jnp.float32)
    mu = jnp.mean(xf, axis=-1, keepdims=True)
    var = jnp.mean(jnp.square(xf - mu), axis=-1, keepdims=True)
    y = (xf - mu) * lax.rsqrt(var + LN_EPS) * g.astype(jnp.float32) + b.astype(jnp.float32)
    return y.astype(x.dtype)


def mixer_inputs(h, w_in):
    b, t, _ = h.shape
    z = jnp.einsum('btd,de->bte', h, w_in)
    q, k, v, u, g_sb, g_pool = jnp.split(z, IN_SPLITS, axis=-1)
    heads = lambda a: a.reshape(b, t, SB_HEADS, SB_HEAD_DIM).transpose(0, 2, 1, 3)
    return heads(q), heads(k), heads(v), u, g_sb, g_pool


def stick_breaking(q, q_pos, k, v, k_pos):
    z = jnp.einsum('bhqd,bhkd->bhqk', q, k, preferred_element_type=jnp.float32) * SB_SCALE
    mask = k_pos[None, :] < q_pos[:, None]
    log_beta = jax.nn.log_sigmoid(z)
    log_rest = jnp.where(mask, jax.nn.log_sigmoid(-z), 0.0)
    suffix = lax.cumsum(log_rest, axis=3, reverse=True) - log_rest
    a = jnp.where(mask, jnp.exp(log_beta + suffix), 0.0)
    return jnp.einsum('bhqk,bhkd->bhqd', a, v.astype(jnp.float32)).astype(v.dtype)


def sb_prompt(q, k, v):
    b, h, s, d = q.shape
    nb = s // Q_BLOCK
    q_blocks = q.reshape(b, h, nb, Q_BLOCK, d).transpose(2, 0, 1, 3, 4)
    pos = jnp.arange(s)
    o = lax.map(lambda args: stick_breaking(args[0], args[1], k, v, pos),
                (q_blocks, pos.reshape(nb, Q_BLOCK)))
    return o.transpose(1, 2, 0, 3, 4).reshape(b, h, s, d)


def pool_mix(hist, valid, w_grp, scale):
    b, n_rows, c = hist.shape
    t = n_rows - POOL_STATE
    vals = hist.astype(jnp.float32) * valid[None, :, None]
    csum = jnp.cumsum(jnp.concatenate([jnp.zeros((b, 1, c), jnp.float32), vals], axis=1), axis=1)
    ccnt = jnp.cumsum(jnp.concatenate([jnp.zeros((1,), jnp.float32), valid]))
    hi = POOL_STATE + 1
    means = []
    for g, w in enumerate(POOL_WINDOWS):
        sl = slice(g * POOL_GROUP_WIDTH, (g + 1) * POOL_GROUP_WIDTH)
        s = csum[:, hi:hi + t, sl] - csum[:, hi - w:hi - w + t, sl]
        cnt = ccnt[hi:hi + t] - ccnt[hi - w:hi - w + t]
        means.append(s / cnt[None, :, None])
    pooled = jnp.concatenate(means, axis=-1) - hist[:, POOL_STATE:].astype(jnp.float32)
    grp = pooled.astype(hist.dtype).reshape(b, t, POOL_GROUPS, POOL_GROUP_WIDTH)
    out = jnp.einsum('btgc,gce->btge', grp, w_grp).reshape(b, t, POOL_WIDTH)
    return out * scale


def merge_branches(o_sb, o_pool, g_sb, g_pool, w_sb_up, w_pool_up, w_o):
    b, h, t, d = o_sb.shape
    o_sb = o_sb.transpose(0, 2, 1, 3).reshape(b, t, SB_WIDTH)
    a = jnp.einsum('btc,cd->btd', o_sb, w_sb_up)
    p = jnp.einsum('btc,cd->btd', o_pool, w_pool_up)
    m = jax.nn.sigmoid(g_sb) * a + jax.nn.sigmoid(g_pool) * p
    return jnp.einsum('btd,de->bte', m, w_o)


def route(h, w_router, router_bias):
    n = h.shape[0]
    per_group = N_EXPERTS // N_EXPERT_GROUPS
    scores = jax.nn.sigmoid(jnp.einsum('nd,de->ne', h, w_router, preferred_element_type=jnp.float32))
    biased = scores + router_bias.astype(jnp.float32)
    group_score = lax.top_k(biased.reshape(n, N_EXPERT_GROUPS, per_group), 2)[0].sum(-1)
    _, top_groups = lax.top_k(group_score, TOPK_GROUPS)
    group_keep = jnp.any(top_groups[:, :, None] == jnp.arange(N_EXPERT_GROUPS)[None, None, :], axis=1)
    expert_keep = jnp.repeat(group_keep, per_group, axis=1)
    _, idx = lax.top_k(jnp.where(expert_keep, biased, -jnp.inf), TOP_K)
    sel = jnp.take_along_axis(scores, idx, axis=1)
    gate = sel / jnp.sum(sel, axis=1, keepdims=True) * ROUTED_SCALE
    return idx, gate


def moe_routed(h, idx, gate, w_eg, w_eu, w_ed):
    n, d = h.shape
    m = n * TOP_K
    e_flat = idx.reshape(m)
    tok_flat = jnp.arange(m, dtype=jnp.int32) // TOP_K
    w_flat = gate.reshape(m)
    order = jnp.argsort(e_flat)
    e_sorted = e_flat[order]
    counts = jnp.zeros((N_EXPERTS,), jnp.int32).at[e_flat].add(1)
    start = jnp.cumsum(counts) - counts
    padded = (counts + MOE_BLOCK - 1) // MOE_BLOCK * MOE_BLOCK
    pend = jnp.cumsum(padded)
    pstart = pend - padded
    dest = pstart[e_sorted] + (jnp.arange(m, dtype=jnp.int32) - start[e_sorted])
    n_blocks = (m + N_EXPERTS * (MOE_BLOCK - 1) + MOE_BLOCK - 1) // MOE_BLOCK
    cap = n_blocks * MOE_BLOCK
    row_tok = jnp.full((cap,), n, jnp.int32).at[dest].set(tok_flat[order])
    row_w = jnp.zeros((cap,), jnp.float32).at[dest].set(w_flat[order])
    block_start = jnp.arange(n_blocks, dtype=jnp.int32) * MOE_BLOCK
    block_exp = jnp.minimum(jnp.searchsorted(pend, block_start, side='right'), N_EXPERTS - 1)
    h_pad = jnp.concatenate([h, jnp.zeros((1, d), h.dtype)], axis=0)

    def body(acc, blk):
        toks, wts, e = blk
        xb = h_pad[toks]
        hid = jax.nn.silu(xb @ w_eg[e]) * (xb @ w_eu[e])
        yb = (hid @ w_ed[e]).astype(jnp.float32) * wts[:, None]
        return acc.at[toks].add(yb), None

    acc, _ = lax.scan(body, jnp.zeros((n + 1, d), jnp.float32),
                      (row_tok.reshape(n_blocks, MOE_BLOCK), row_w.reshape(n_blocks, MOE_BLOCK), block_exp))
    return acc[:n].astype(h.dtype)


def shared_expert(h, w_g, w_u, w_d):
    return (jax.nn.silu(h @ w_g) * (h @ w_u)) @ w_d


def setup_inputs(seed: int = 0) -> dict:
    key = jax.random.key(seed)
    ks = jax.random.split(key, 25)
    f32 = jnp.float32

    def nrm(k, shape, scale):
        return jax.random.normal(k, shape, f32) * scale

    L = DEPTH
    return {
        'x_prompt': nrm(ks[0], (BATCH, SEQ, D_MODEL), 1.0),
        'x_sample': nrm(ks[1], (DEC_BATCH, DEC_SEQ, D_MODEL), 1.0),
        'cache_k': nrm(ks[2], (L, DEC_BATCH, SB_HEADS, PAST_LEN, SB_HEAD_DIM), 1.0),
        'cache_v': nrm(ks[3], (L, DEC_BATCH, SB_HEADS, PAST_LEN, SB_HEAD_DIM), 1.0),
        'state_pool': nrm(ks[4], (L, DEC_BATCH, POOL_STATE, POOL_WIDTH), 1.0),
        'ln0_g': 1.0 + nrm(ks[5], (D_MODEL,), 0.05),
        'ln0_b': nrm(ks[6], (D_MODEL,), 0.02),
        'w_in': nrm(ks[7], (L, D_MODEL, IN_WIDTH), D_MODEL ** -0.5),
        'w_sb_up': nrm(ks[8], (L, SB_WIDTH, D_MODEL), SB_WIDTH ** -0.5),
        'w_pool_grp': nrm(ks[9], (L, POOL_GROUPS, POOL_GROUP_WIDTH, POOL_GROUP_WIDTH), POOL_GROUP_WIDTH ** -0.5),
        'pool_scale': 1.0 + nrm(ks[10], (L, POOL_WIDTH), 0.1),
        'w_pool_up': nrm(ks[11], (L, POOL_WIDTH, D_MODEL), POOL_WIDTH ** -0.5),
        'w_o': nrm(ks[12], (L, D_MODEL, D_MODEL), DEEPNORM_BETA * D_MODEL ** -0.5),
        'ln1_g': 1.0 + nrm(ks[13], (L, D_MODEL), 0.05),
        'ln1_b': nrm(ks[14], (L, D_MODEL), 0.02),
        'w_router': nrm(ks[15], (L, D_MODEL, N_EXPERTS), D_MODEL ** -0.5),
        'router_bias': nrm(ks[16], (L, N_EXPERTS), 0.01),
        'w_exp_gate': nrm(ks[17], (L, N_EXPERTS, D_MODEL, EXPERT_DIM), D_MODEL ** -0.5),
        'w_exp_up': nrm(ks[18], (L, N_EXPERTS, D_MODEL, EXPERT_DIM), D_MODEL ** -0.5),
        'w_exp_down': nrm(ks[19], (L, N_EXPERTS, EXPERT_DIM, D_MODEL), DEEPNORM_BETA * EXPERT_DIM ** -0.5),
        'w_sh_gate': nrm(ks[20], (L, D_MODEL, SHARED_DIM), D_MODEL ** -0.5),
        'w_sh_up': nrm(ks[21], (L, D_MODEL, SHARED_DIM), D_MODEL ** -0.5),
        'w_sh_down': nrm(ks[22], (L, SHARED_DIM, D_MODEL), DEEPNORM_BETA * SHARED_DIM ** -0.5),
        'ln2_g': 1.0 + nrm(ks[23], (L, D_MODEL), 0.05),
        'ln2_b': nrm(ks[24], (L, D_MODEL), 0.02),
    }


def reference(x_prompt, x_sample, cache_k, cache_v, state_pool, ln0_g, ln0_b, w_in, w_sb_up,
              w_pool_grp, pool_scale, w_pool_up, w_o, ln1_g, ln1_b, w_router, router_bias,
              w_exp_gate, w_exp_up, w_exp_down, w_sh_gate, w_sh_up, w_sh_down, ln2_g, ln2_b):
    b_p, t_p, _ = x_prompt.shape
    t_s = x_sample.shape[1]
    past = cache_k.shape[3]
    n_prompt = b_p * t_p
    hp = layer_norm(x_prompt, ln0_g, ln0_b)
    hs = layer_norm(x_sample, ln0_g, ln0_b)
    valid_p = jnp.concatenate([jnp.zeros((POOL_STATE,), jnp.float32), jnp.ones((t_p,), jnp.float32)])
    valid_s = jnp.ones((POOL_STATE + t_s,), jnp.float32)
    q_pos_s = past + jnp.arange(t_s)
    k_pos_s = jnp.arange(past + t_s)
    kp_l, vp_l, pp_l, ks_l, vs_l, ps_l = [], [], [], [], [], []
    for l in range(DEPTH):
        qp, kp, vp, up, gsb_p, gpool_p = mixer_inputs(hp, w_in[l])
        o_sb_p = sb_prompt(qp, kp, vp)
        hist_p = jnp.concatenate([jnp.zeros((b_p, POOL_STATE, POOL_WIDTH), up.dtype), up], axis=1)
        o_pool_p = pool_mix(hist_p, valid_p, w_pool_grp[l], pool_scale[l])
        mix_p = merge_branches(o_sb_p, o_pool_p, gsb_p, gpool_p, w_sb_up[l], w_pool_up[l], w_o[l])
        hp = layer_norm(DEEPNORM_ALPHA * hp + mix_p, ln1_g[l], ln1_b[l])
        qs, ks, vs, us, gsb_s, gpool_s = mixer_inputs(hs, w_in[l])
        k_all = jnp.concatenate([cache_k[l], ks], axis=2)
        v_all = jnp.concatenate([cache_v[l], vs], axis=2)
        o_sb_s = stick_breaking(qs, q_pos_s, k_all, v_all, k_pos_s)
        hist_s = jnp.concatenate([state_pool[l], us], axis=1)
        o_pool_s = pool_mix(hist_s, valid_s, w_pool_grp[l], pool_scale[l])
        mix_s = merge_branches(o_sb_s, o_pool_s, gsb_s, gpool_s, w_sb_up[l], w_pool_up[l], w_o[l])
        hs = layer_norm(DEEPNORM_ALPHA * hs + mix_s, ln1_g[l], ln1_b[l])
        flat = jnp.concatenate([hp.reshape(-1, D_MODEL), hs.reshape(-1, D_MODEL)], axis=0)
        idx, gate = route(flat, w_router[l], router_bias[l])
        ffn = (moe_routed(flat, idx, gate, w_exp_gate[l], w_exp_up[l], w_exp_down[l])
               + shared_expert(flat, w_sh_gate[l], w_sh_up[l], w_sh_down[l]))
        hp = layer_norm(DEEPNORM_ALPHA * hp + ffn[:n_prompt].reshape(hp.shape), ln2_g[l], ln2_b[l])
        hs = layer_norm(DEEPNORM_ALPHA * hs + ffn[n_prompt:].reshape(hs.shape), ln2_g[l], ln2_b[l])
        kp_l.append(kp)
        vp_l.append(vp)
        pp_l.append(hist_p[:, -POOL_STATE:])
        ks_l.append(ks)
        vs_l.append(vs)
        ps_l.append(hist_s[:, -POOL_STATE:])
    return (hp, hs, jnp.stack(kp_l), jnp.stack(vp_l), jnp.stack(pp_l),
            jnp.stack(ks_l), jnp.stack(vs_l), jnp.stack(ps_l))
```

```python
import functools

import jax
import jax.numpy as jnp
from jax import lax
from jax.experimental import pallas as pl
from jax.experimental.pallas import tpu as pltpu

F32 = jnp.float32
BF16 = jnp.bfloat16
I32 = jnp.int32
U32 = jnp.uint32

D_MODEL = 1024
SB_HEADS = 8
SB_HEAD_DIM = 64
SB_WIDTH = SB_HEADS * SB_HEAD_DIM
SB_SCALE = SB_HEAD_DIM ** -0.5
POOL_WINDOWS = (2, 4, 8, 16)
POOL_WIDTH = D_MODEL // 2
POOL_GROUP_WIDTH = POOL_WIDTH // len(POOL_WINDOWS)
POOL_STATE = max(POOL_WINDOWS) - 1
HALO = POOL_STATE + 1
N_EXPERTS = 256
TOP_K = 8
N_EXPERT_GROUPS = 8
TOPK_GROUPS = 4
GROUP_SIZE = N_EXPERTS // N_EXPERT_GROUPS
EXPERT_DIM = D_MODEL // 4
ROUTED_SCALE = 2.5
LN_EPS = 1e-5
DEPTH = 1
DEEPNORM_ALPHA = (2 * DEPTH) ** 0.25

HALF = D_MODEL // 2
KEY_TILE = 128
MAX_QUERY_TILE = 256
EXPERT_BLOCK = 256
TOKEN_TILE = 512
GATHER_TILE = 256
EXP_UNDERFLOW = 104.0
VMEM_LIMIT = 56 * 1024 * 1024

_NT = (((1,), (1,)), ((), ()))


def _params(*semantics):
    return pltpu.CompilerParams(dimension_semantics=semantics, vmem_limit_bytes=VMEM_LIMIT)


def _layer_norm(x, g, b):
    mu = jnp.mean(x, axis=-1, keepdims=True)
    xc = x - mu
    var = jnp.mean(xc * xc, axis=-1, keepdims=True)
    return xc * lax.rsqrt(var + LN_EPS) * g + b


def _sigmoid(x):
    return 1.0 / (1.0 + jnp.exp(-x))


def _dot(a, b):
    return jnp.dot(a, b, preferred_element_type=F32)


def _pack_halves(x):
    bits = lax.bitcast_convert_type(x.astype(BF16).astype(F32), U32)
    return (bits[:, HALF:] & jnp.uint32(0xFFFF0000)) | (bits[:, :HALF] >> 16)


def _unpack_halves(pk):
    lo = lax.bitcast_convert_type(pk << 16, F32)
    hi = lax.bitcast_convert_type(pk & jnp.uint32(0xFFFF0000), F32)
    return lo, hi


def _proj_kernel(x_ref, g_ref, b_ref, w_ref, qb_ref, kb_ref, vb_ref, k_ref, v_ref, u_ref, sg_ref):
    h = _layer_norm(x_ref[0], g_ref[...], b_ref[...]).astype(BF16)

    def proj(lo, hi):
        return _dot(h, w_ref[:, lo:hi])

    qb_ref[0] = (proj(0, SB_WIDTH) * SB_SCALE).astype(BF16)
    zk = proj(SB_WIDTH, 2 * SB_WIDTH)
    kb_ref[0] = zk.astype(BF16)
    zv = proj(2 * SB_WIDTH, 3 * SB_WIDTH)
    vb_ref[0] = zv.astype(BF16)
    for hd in range(SB_HEADS):
        sl = slice(hd * SB_HEAD_DIM, (hd + 1) * SB_HEAD_DIM)
        k_ref[0, 0, hd] = zk[:, sl]
        v_ref[0, 0, hd] = zv[:, sl]
    u0 = 3 * SB_WIDTH
    u_ref[0] = proj(u0, u0 + POOL_WIDTH)
    g0 = u0 + POOL_WIDTH
    for c in range(2 * D_MODEL // SB_WIDTH):
        sl = slice(c * SB_WIDTH, (c + 1) * SB_WIDTH)
        sg_ref[0, :, sl] = _sigmoid(proj(g0 + c * SB_WIDTH, g0 + (c + 1) * SB_WIDTH)).astype(BF16)


def _proj(x, ln_g, ln_b, w_in_b):
    b, t, d = x.shape
    tm = min(TOKEN_TILE, t)
    in_width = w_in_b.shape[1]
    row = lambda width: pl.BlockSpec((1, tm, width), lambda i, j: (i, j, 0))
    heads = pl.BlockSpec((1, 1, SB_HEADS, tm, SB_HEAD_DIM), lambda i, j: (0, i, 0, j, 0))
    const = lambda shape: pl.BlockSpec(shape, lambda i, j: (0,) * len(shape))
    head_shape = jax.ShapeDtypeStruct((1, b, SB_HEADS, t, SB_HEAD_DIM), F32)
    return pl.pallas_call(
        _proj_kernel,
        grid=(b, t // tm),
        in_specs=[row(d), const((1, d)), const((1, d)), const((d, in_width))],
        out_specs=[row(SB_WIDTH), row(SB_WIDTH), row(SB_WIDTH), heads, heads, row(POOL_WIDTH), row(2 * d)],
        out_shape=[
            jax.ShapeDtypeStruct((b, t, SB_WIDTH), BF16),
            jax.ShapeDtypeStruct((b, t, SB_WIDTH), BF16),
            jax.ShapeDtypeStruct((b, t, SB_WIDTH), BF16),
            head_shape,
            head_shape,
            jax.ShapeDtypeStruct((b, t, POOL_WIDTH), F32),
            jax.ShapeDtypeStruct((b, t, 2 * d), BF16),
        ],
        compiler_params=_params("parallel", "parallel"),
        name="proj",
    )(x, ln_g.reshape(1, d), ln_b.reshape(1, d), w_in_b)


def _attn_kernel(q_ref, k_ref, v_ref, tri_ref, o_ref, acc_ref, carry_ref, *, tq, q_pos0):
    tk = KEY_TILE
    q_first = q_pos0 + pl.program_id(1) * tq
    acc_ref[...] = jnp.zeros_like(acc_ref)
    carry_ref[...] = jnp.zeros_like(carry_ref)
    q_pos = q_first + lax.broadcasted_iota(I32, (tq, tk), 0)
    k_off = lax.broadcasted_iota(I32, (tq, tk), 1)
    low_half = lax.broadcasted_iota(I32, (1, 2 * SB_HEAD_DIM), 1) < SB_HEAD_DIM
    tri = tri_ref[...]

    def step(state):
        j, _ = state
        k0 = pl.multiple_of(j * tk, tk)
        mask = (k0 + k_off) < q_pos
        worst = jnp.full((tq, tk), -jnp.inf, F32)
        for pair in range(SB_HEADS // 2):
            lanes = slice(pair * 2 * SB_HEAD_DIM, (pair + 1) * 2 * SB_HEAD_DIM)
            q2 = q_ref[0, :, lanes]
            k2 = k_ref[0, pl.ds(k0, tk), lanes]
            v2 = v_ref[0, pl.ds(k0, tk), lanes]
            out = acc_ref[:, lanes]
            for sub in range(2):
                head = 2 * pair + sub
                own = low_half if sub == 0 else jnp.logical_not(low_half)
                s = lax.dot_general(jnp.where(own, q2, jnp.zeros_like(q2)), k2, _NT,
                                    preferred_element_type=F32)
                soft = jnp.log(1.0 + jnp.exp(-jnp.abs(s)))
                log_beta = jnp.minimum(s, 0.0) - soft
                log_rest = jnp.where(mask, -jnp.maximum(s, 0.0) - soft, 0.0)
                hi = log_rest.astype(BF16)
                lo = (log_rest - hi.astype(F32)).astype(BF16)
                sums = _dot(hi, tri) + _dot(lo, tri)
                carry = carry_ref[head]
                a = jnp.where(mask, jnp.exp(log_beta + sums[:, :tk] + carry), 0.0)
                carry = carry + sums[:, tk:]
                carry_ref[head] = carry
                worst = jnp.maximum(worst, carry)
                out = out + _dot(a.astype(BF16), jnp.where(own, v2, jnp.zeros_like(v2)))
            acc_ref[:, lanes] = out
        return j - 1, (jnp.max(worst) > -EXP_UNDERFLOW).astype(I32)

    j_first = (q_first + tq - 1) // tk
    lax.while_loop(lambda st: jnp.logical_and(st[0] >= 0, st[1] > 0), step, (j_first, jnp.int32(1)))
    o_ref[0] = acc_ref[...].astype(BF16)


def _attn(qb, kb, vb, *, q_pos0):
    b, t, w = qb.shape
    s_k = kb.shape[1]
    tq = min(MAX_QUERY_TILE, t)
    tk = KEY_TILE
    assert t % tq == 0 and s_k % tk == 0 and s_k >= q_pos0 + t
    r = lax.broadcasted_iota(I32, (tk, 2 * tk), 0)
    c = lax.broadcasted_iota(I32, (tk, 2 * tk), 1)
    tri = jnp.logical_or(r > c, c >= tk).astype(BF16)
    return pl.pallas_call(
        functools.partial(_attn_kernel, tq=tq, q_pos0=q_pos0),
        grid=(b, t // tq),
        in_specs=[
            pl.BlockSpec((1, tq, w), lambda i, j: (i, j, 0)),
            pl.BlockSpec((1, s_k, w), lambda i, j: (i, 0, 0)),
            pl.BlockSpec((1, s_k, w), lambda i, j: (i, 0, 0)),
            pl.BlockSpec((tk, 2 * tk), lambda i, j: (0, 0)),
        ],
        out_specs=pl.BlockSpec((1, tq, w), lambda i, j: (i, j, 0)),
        out_shape=jax.ShapeDtypeStruct((b, t, w), BF16),
        scratch_shapes=[pltpu.VMEM((tq, w), F32), pltpu.VMEM((SB_HEADS, tq, tk), F32)],
        compiler_params=_params("parallel", "parallel"),
        name="attn",
    )(qb, kb, vb, tri)


def _merge_kernel(x_ref, osb_ref, u_ref, uprev_ref, hist_ref, sg_ref, g0_ref, b0_ref, wsb_ref, wgrp_ref,
                  pscale_ref, wpool_ref, wo_ref, g1_ref, b1_ref, h_ref, hpk_ref, ext_ref, *, tm, hist_valid):
    t = pl.program_id(1)
    h0 = _layer_norm(x_ref[0], g0_ref[...], b0_ref[...])
    u = u_ref[0]
    ext_ref[:HALO, :] = jnp.where(t == 0, hist_ref[0], uprev_ref[0])
    ext_ref[HALO:, :] = u
    rows_seen = t * tm + lax.broadcasted_iota(I32, (tm, 1), 0) + (1 + hist_valid)
    pool_out = []
    for g, window in enumerate(POOL_WINDOWS):
        lanes = slice(g * POOL_GROUP_WIDTH, (g + 1) * POOL_GROUP_WIDTH)
        total = u[:, lanes]
        for back in range(1, window):
            total = total + ext_ref[HALO - back:HALO - back + tm, lanes]
        count = jnp.minimum(rows_seen, window).astype(F32)
        pooled = total / count - u[:, lanes]
        pool_out.append(_dot(pooled.astype(BF16), wgrp_ref[g]))
    o_pool = jnp.concatenate(pool_out, axis=1) * pscale_ref[...]
    a = _dot(osb_ref[0], wsb_ref[...])
    p = _dot(o_pool.astype(BF16), wpool_ref[...])
    sg = sg_ref[0].astype(F32)
    m = sg[:, :D_MODEL] * a + sg[:, D_MODEL:] * p
    mix = _dot(m.astype(BF16), wo_ref[...])
    h1 = _layer_norm(DEEPNORM_ALPHA * h0 + mix, g1_ref[...], b1_ref[...])
    h_ref[...] = h1
    hpk_ref[...] = _pack_halves(h1)


def _merge(x, osb, u, hist, sg, ln0_g, ln0_b, wsb, wgrp, pscale, wpool, wo, ln1_g, ln1_b, *, hist_valid):
    b, t, d = x.shape
    tm = min(TOKEN_TILE, t)
    nt = t // tm
    row = lambda width: pl.BlockSpec((1, tm, width), lambda i, j: (i, j, 0))
    const = lambda shape: pl.BlockSpec(shape, lambda i, j: (0,) * len(shape))
    flat = lambda width: pl.BlockSpec((tm, width), lambda i, j: (i * nt + j, 0))
    prev = pl.BlockSpec((1, HALO, POOL_WIDTH), lambda i, j: (i, jnp.maximum(j * (tm // HALO) - 1, 0), 0))
    return pl.pallas_call(
        functools.partial(_merge_kernel, tm=tm, hist_valid=hist_valid),
        grid=(b, nt),
        in_specs=[
            row(d), row(SB_WIDTH), row(POOL_WIDTH), prev,
            pl.BlockSpec((1, HALO, POOL_WIDTH), lambda i, j: (i, 0, 0)),
            row(2 * d), const((1, d)), const((1, d)), const((SB_WIDTH, d)),
            const((len(POOL_WINDOWS), POOL_GROUP_WIDTH, POOL_GROUP_WIDTH)), const((1, POOL_WIDTH)),
            const((POOL_WIDTH, d)), const((d, d)), const((1, d)), const((1, d)),
        ],
        out_specs=[flat(d), flat(HALF)],
        out_shape=[jax.ShapeDtypeStruct((b * t, d), F32), jax.ShapeDtypeStruct((b * t, HALF), U32)],
        scratch_shapes=[pltpu.VMEM((HALO + tm, POOL_WIDTH), F32)],
        compiler_params=_params("parallel", "parallel"),
        name="merge",
    )(x, osb, u, u, hist, sg, ln0_g.reshape(1, d), ln0_b.reshape(1, d), wsb, wgrp, pscale.reshape(1, -1),
      wpool, wo, ln1_g.reshape(1, d), ln1_b.reshape(1, d))


def _first_max(vals, index, n):
    top = jnp.max(vals, axis=0, keepdims=True)
    first = jnp.min(jnp.where(vals == top, index, n), axis=0, keepdims=True)
    return top, first


def _route_kernel(hpk_ref, wr_ref, bias_ref, tri_ref, cin_ref, idx_ref, gate_ref, rank_ref, cout_ref, *, tm):
    @pl.when(pl.program_id(0) == 0)
    def _():
        cout_ref[...] = cin_ref[...]

    lo, hi = _unpack_halves(hpk_ref[...])
    logits = (lax.dot_general(wr_ref[:, :HALF], lo.astype(BF16), _NT, preferred_element_type=F32)
              + lax.dot_general(wr_ref[:, HALF:], hi.astype(BF16), _NT, preferred_element_type=F32))
    scores = _sigmoid(logits)
    biased = scores + bias_ref[...]
    neg = jnp.float32(-jnp.inf)

    in_group = lax.broadcasted_iota(I32, (GROUP_SIZE, tm), 0)
    group_scores = []
    for g in range(N_EXPERT_GROUPS):
        blk = biased[g * GROUP_SIZE:(g + 1) * GROUP_SIZE, :]
        top, first = _first_max(blk, in_group, GROUP_SIZE)
        second = jnp.max(jnp.where(in_group == first, neg, blk), axis=0, keepdims=True)
        group_scores.append(top + second)
    remaining = jnp.concatenate(group_scores, axis=0)
    group_id = lax.broadcasted_iota(I32, (N_EXPERT_GROUPS, tm), 0)
    kept = jnp.zeros((N_EXPERT_GROUPS, tm), F32)
    for _ in range(TOPK_GROUPS):
        _, first = _first_max(remaining, group_id, N_EXPERT_GROUPS)
        hit = group_id == first
        kept = jnp.where(hit, 1.0, kept)
        remaining = jnp.where(hit, neg, remaining)
    remaining = jnp.concatenate(
        [jnp.where(kept[g:g + 1, :] > 0.5, biased[g * GROUP_SIZE:(g + 1) * GROUP_SIZE, :], neg)
         for g in range(N_EXPERT_GROUPS)], axis=0)

    expert_id = lax.broadcasted_iota(I32, (N_EXPERTS, tm), 0)
    chosen, picks, picked_scores = jnp.zeros((N_EXPERTS, tm), F32), [], []
    for _ in range(TOP_K):
        _, first = _first_max(remaining, expert_id, N_EXPERTS)
        hit = expert_id == first
        picks.append((first, hit))
        picked_scores.append(jnp.sum(jnp.where(hit, scores, 0.0), axis=0, keepdims=True))
        chosen = jnp.where(hit, 1.0, chosen)
        remaining = jnp.where(hit, neg, remaining)
    denom = picked_scores[0]
    for sc in picked_scores[1:]:
        denom = denom + sc
    gate_ref[...] = jnp.concatenate([sc / denom * ROUTED_SCALE for sc in picked_scores], axis=0)
    idx_ref[...] = jnp.concatenate([first for first, _ in picks], axis=0)

    slot = cout_ref[...] + _dot(chosen.astype(BF16), tri_ref[...])
    rank_ref[...] = jnp.concatenate(
        [jnp.sum(jnp.where(hit, slot, 0.0), axis=0, keepdims=True) for _, hit in picks], axis=0).astype(I32)
    cout_ref[...] += jnp.sum(chosen, axis=1, keepdims=True)


def _route(hpk, wr_t, bias, count_in):
    n = hpk.shape[0]
    tm = min(TOKEN_TILE, n)
    assert n % tm == 0
    earlier = (lax.broadcasted_iota(I32, (tm, tm), 0) < lax.broadcasted_iota(I32, (tm, tm), 1)).astype(BF16)
    const = lambda shape: pl.BlockSpec(shape, lambda i: (0,) * len(shape))
    picks = pl.BlockSpec((TOP_K, tm), lambda i: (0, i))
    return pl.pallas_call(
        functools.partial(_route_kernel, tm=tm),
        grid=(n // tm,),
        in_specs=[pl.BlockSpec((tm, HALF), lambda i: (i, 0)), const((N_EXPERTS, D_MODEL)), const((N_EXPERTS, 1)),
                  const((tm, tm)), const((N_EXPERTS, 1))],
        out_specs=[picks, picks, picks, const((N_EXPERTS, 1))],
        out_shape=[jax.ShapeDtypeStruct((TOP_K, n), I32), jax.ShapeDtypeStruct((TOP_K, n), F32),
                   jax.ShapeDtypeStruct((TOP_K, n), I32), jax.ShapeDtypeStruct((N_EXPERTS, 1), F32)],
        compiler_params=_params("arbitrary"),
        name="route",
    )(hpk, wr_t, bias, earlier, count_in)


def _row_copy(src_ref, src_row, dst_ref, dst_row, sem):
    return pltpu.make_async_copy(src_ref.at[pl.ds(src_row, 1), :], dst_ref.at[pl.ds(dst_row, 1), :], sem)


def _dispatch_kernel(dest_ref, hpk_ref, *rest, tm):
    xs_ref, sem = rest[-2], rest[-1]

    def start(tok, carry):
        for j in range(TOP_K):
            _row_copy(hpk_ref, tok, xs_ref, dest_ref[j, tok], sem).start()
        return carry

    def wait(tok, carry):
        for j in range(TOP_K):
            _row_copy(hpk_ref, 0, xs_ref, 0, sem).wait()
        return carry

    lax.fori_loop(0, tm, start, 0)
    lax.fori_loop(0, tm, wait, 0)


def _dispatch(dest, hpk, xs, cap):
    n = hpk.shape[0]
    tm = min(GATHER_TILE, n)
    assert n % tm == 0
    in_specs = [pl.BlockSpec((TOP_K, tm), lambda i: (0, i), memory_space=pltpu.SMEM),
                pl.BlockSpec((tm, HALF), lambda i: (i, 0))]
    args = [dest, hpk]
    aliases = {}
    if xs is not None:
        in_specs.append(pl.BlockSpec(memory_space=pl.ANY))
        args.append(xs)
        aliases = {2: 0}
    return pl.pallas_call(
        functools.partial(_dispatch_kernel, tm=tm),
        grid=(n // tm,),
        in_specs=in_specs,
        out_specs=pl.BlockSpec(memory_space=pl.ANY),
        out_shape=jax.ShapeDtypeStruct((cap, HALF), U32),
        scratch_shapes=[pltpu.SemaphoreType.DMA],
        input_output_aliases=aliases,
        compiler_params=pltpu.CompilerParams(dimension_semantics=("arbitrary",), vmem_limit_bytes=VMEM_LIMIT,
                                             has_side_effects=True),
        name="dispatch",
    )(*args)


def _expert_kernel(bexp_ref, brow_ref, bvalid_ref, x_ref, wg_ref, wu_ref, wd_ref, y_ref, wg_b, wu_b, wd_b):
    i = pl.program_id(0)
    fresh = jnp.logical_or(i == 0, bexp_ref[i] != bexp_ref[jnp.maximum(i - 1, 0)])

    @pl.when(fresh)
    def _():
        wg_b[...] = wg_ref[0].astype(BF16)
        wu_b[...] = wu_ref[0].astype(BF16)
        wd_b[...] = wd_ref[0].astype(BF16)

    @pl.when(bvalid_ref[i] == 1)
    def _():
        lo, hi = _unpack_halves(x_ref[...])
        lo, hi = lo.astype(BF16), hi.astype(BF16)
        gate = _dot(lo, wg_b[:HALF, :]) + _dot(hi, wg_b[HALF:, :])
        up = _dot(lo, wu_b[:HALF, :]) + _dot(hi, wu_b[HALF:, :])
        hid = gate * _sigmoid(gate) * up
        y_ref[...] = _pack_halves(_dot(hid.astype(BF16), wd_b[...]))


def _experts(bexp, brow, bvalid, xs, w_gate, w_up, w_down):
    cap = xs.shape[0]
    blk = EXPERT_BLOCK
    rows = pl.BlockSpec((blk, HALF), lambda i, be, br, bv: (br[i], 0))
    return pl.pallas_call(
        _expert_kernel,
        grid_spec=pltpu.PrefetchScalarGridSpec(
            num_scalar_prefetch=3,
            grid=(cap // blk,),
            in_specs=[rows,
                      pl.BlockSpec((1, D_MODEL, EXPERT_DIM), lambda i, be, br, bv: (be[i], 0, 0)),
                      pl.BlockSpec((1, D_MODEL, EXPERT_DIM), lambda i, be, br, bv: (be[i], 0, 0)),
                      pl.BlockSpec((1, EXPERT_DIM, D_MODEL), lambda i, be, br, bv: (be[i], 0, 0))],
            out_specs=rows,
            scratch_shapes=[pltpu.VMEM((D_MODEL, EXPERT_DIM), BF16), pltpu.VMEM((D_MODEL, EXPERT_DIM), BF16),
                            pltpu.VMEM((EXPERT_DIM, D_MODEL), BF16)],
        ),
        out_shape=jax.ShapeDtypeStruct((cap, HALF), U32),
        compiler_params=_params("arbitrary"),
        name="experts",
    )(bexp, brow, bvalid, xs, w_gate, w_up, w_down)


def _combine_kernel(dest_ref, h_ref, gate_ref, ys_ref, wsg_ref, wsu_ref, wsd_ref, g_ref, b_ref, o_ref,
                    rows_ref, sem, *, tm):
    def start(tok, carry):
        for j in range(TOP_K):
            _row_copy(ys_ref, dest_ref[j, tok], rows_ref.at[j], tok, sem).start()
        return carry

    def wait(tok, carry):
        for j in range(TOP_K):
            _row_copy(ys_ref, 0, rows_ref.at[j], 0, sem).wait()
        return carry

    lax.fori_loop(0, tm, start, 0)
    h = h_ref[...]
    hb = h.astype(BF16)
    gate = _dot(hb, wsg_ref[...])
    hid = gate * _sigmoid(gate) * _dot(hb, wsu_ref[...])
    shared = _dot(hid.astype(BF16), wsd_ref[...])
    lax.fori_loop(0, tm, wait, 0)
    weights = gate_ref[...]
    ffn_lo = jnp.zeros((tm, HALF), F32)
    ffn_hi = jnp.zeros((tm, HALF), F32)
    for j in range(TOP_K):
        lo, hi = _unpack_halves(rows_ref[j])
        w = weights[:, j:j + 1]
        ffn_lo = ffn_lo + w * lo
        ffn_hi = ffn_hi + w * hi
    ffn = jnp.concatenate([ffn_lo, ffn_hi], axis=1) + shared
    o_ref[...] = _layer_norm(DEEPNORM_ALPHA * h + ffn, g_ref[...], b_ref[...])


def _combine(dest, h, gate_t, ys, wsg, wsu, wsd, ln_g, ln_b):
    n, d = h.shape
    tm = min(GATHER_TILE, n)
    assert n % tm == 0
    const = lambda shape: pl.BlockSpec(shape, lambda i: (0,) * len(shape))
    return pl.pallas_call(
        functools.partial(_combine_kernel, tm=tm),
        grid=(n // tm,),
        in_specs=[pl.BlockSpec((TOP_K, tm), lambda i: (0, i), memory_space=pltpu.SMEM),
                  pl.BlockSpec((tm, d), lambda i: (i, 0)),
                  pl.BlockSpec((tm, TOP_K), lambda i: (i, 0)),
                  pl.BlockSpec(memory_space=pl.ANY),
                  const((d, EXPERT_DIM)), const((d, EXPERT_DIM)), const((EXPERT_DIM, d)),
                  const((1, d)), const((1, d))],
        out_specs=pl.BlockSpec((tm, d), lambda i: (i, 0)),
        out_shape=jax.ShapeDtypeStruct((n, d), F32),
        scratch_shapes=[pltpu.VMEM((TOP_K, tm, HALF), U32), pltpu.SemaphoreType.DMA],
        compiler_params=_params("arbitrary"),
        name="combine",
    )(dest, h, gate_t, ys, wsg, wsu, wsd, ln_g.reshape(1, d), ln_b.reshape(1, d))


def _key_major(cache):
    b, h, s, hd = cache.shape
    return cache.transpose(0, 2, 1, 3).reshape(b, s, h * hd).astype(BF16)


def kernel(x_prompt, x_sample, cache_k, cache_v, state_pool, ln0_g, ln0_b, w_in, w_sb_up, w_pool_grp, pool_scale,
           w_pool_up, w_o, ln1_g, ln1_b, w_router, router_bias, w_exp_gate, w_exp_up, w_exp_down, w_sh_gate,
           w_sh_up, w_sh_down, ln2_g, ln2_b):
    assert w_in.shape[0] == DEPTH
    b_p, t_p, d = x_prompt.shape
    b_s, t_s, _ = x_sample.shape
    past = cache_k.shape[3]
    n_p, n_s = b_p * t_p, b_s * t_s
    layer = 0
    w_in_b = w_in[layer].astype(BF16)
    mix_w = (w_sb_up[layer].astype(BF16), w_pool_grp[layer].astype(BF16), pool_scale[layer],
             w_pool_up[layer].astype(BF16), w_o[layer].astype(BF16), ln1_g[layer], ln1_b[layer])

    qb, kb, vb, k_p, v_p, u_p, sg_p = _proj(x_prompt, ln0_g, ln0_b, w_in_b)
    osb_p = _attn(qb, kb, vb, q_pos0=0)
    hist_p = jnp.zeros((b_p, HALO, POOL_WIDTH), F32)
    h_p, hpk_p = _merge(x_prompt, osb_p, u_p, hist_p, sg_p, ln0_g, ln0_b, *mix_w, hist_valid=0)

    qs, ks, vs, k_s, v_s, u_s, sg_s = _proj(x_sample, ln0_g, ln0_b, w_in_b)
    key_rows = -(-(past + t_s) // KEY_TILE) * KEY_TILE
    pad = ((0, 0), (0, key_rows - past - t_s), (0, 0))
    k_all = jnp.pad(jnp.concatenate([_key_major(cache_k[layer]), ks], axis=1), pad)
    v_all = jnp.pad(jnp.concatenate([_key_major(cache_v[layer]), vs], axis=1), pad)
    osb_s = _attn(qs, k_all, v_all, q_pos0=past)
    hist_s = jnp.concatenate([jnp.zeros((b_s, 1, POOL_WIDTH), F32), state_pool[layer]], axis=1)
    h_s, hpk_s = _merge(x_sample, osb_s, u_s, hist_s, sg_s, ln0_g, ln0_b, *mix_w, hist_valid=POOL_STATE)

    wr_t = w_router[layer].T.astype(BF16)
    bias = router_bias[layer].astype(F32).reshape(N_EXPERTS, 1)
    idx_p, gate_p, rank_p, count = _route(hpk_p, wr_t, bias, jnp.zeros((N_EXPERTS, 1), F32))
    idx_s, gate_s, rank_s, count = _route(hpk_s, wr_t, bias, count)

    blk = EXPERT_BLOCK
    n_blocks = ((n_p + n_s) * TOP_K + N_EXPERTS * (blk - 1) + blk - 1) // blk
    cap = n_blocks * blk
    counts = count[:, 0].astype(I32)
    padded = (counts + blk - 1) // blk * blk
    pend = jnp.cumsum(padded)
    pstart = pend - padded
    dest_p = pstart[idx_p] + rank_p
    dest_s = pstart[idx_s] + rank_s
    used = pend[-1] // blk
    block_id = jnp.arange(n_blocks, dtype=I32)
    brow = jnp.minimum(block_id, used - 1)
    bexp = jnp.minimum(jnp.searchsorted(pend, brow * blk, side='right'), N_EXPERTS - 1).astype(I32)
    bvalid = (block_id < used).astype(I32)

    xs = _dispatch(dest_p, hpk_p, None, cap)
    xs = _dispatch(dest_s, hpk_s, xs, cap)
    ys = _experts(bexp, brow, bvalid, xs, w_exp_gate[layer], w_exp_up[layer], w_exp_down[layer])

    shared_w = (w_sh_gate[layer].astype(BF16), w_sh_up[layer].astype(BF16), w_sh_down[layer].astype(BF16),
                ln2_g[layer], ln2_b[layer])
    y_p = _combine(dest_p, h_p, gate_p.T, ys, *shared_w)
    y_s = _combine(dest_s, h_s, gate_s.T, ys, *shared_w)

    new_pool_p = u_p[:, t_p - POOL_STATE:][None]
    new_pool_s = jnp.concatenate([state_pool[layer], u_s], axis=1)[:, -POOL_STATE:][None]
    return (y_p.reshape(b_p, t_p, d), y_s.reshape(b_s, t_s, d), k_p, v_p, new_pool_p, k_s, v_s, new_pool_s)
```

```python
import functools

import jax
import jax.numpy as jnp
from jax import lax
from jax.experimental import pallas as pl
from jax.experimental.pallas import tpu as pltpu

F32 = jnp.float32
BF16 = jnp.bfloat16
I32 = jnp.int32
U32 = jnp.uint32

D_MODEL = 1024
SB_HEADS = 8
SB_HEAD_DIM = 64
SB_WIDTH = SB_HEADS * SB_HEAD_DIM
SB_SCALE = SB_HEAD_DIM ** -0.5
POOL_WINDOWS = (2, 4, 8, 16)
POOL_WIDTH = D_MODEL // 2
POOL_GROUP_WIDTH = POOL_WIDTH // len(POOL_WINDOWS)
POOL_STATE = max(POOL_WINDOWS) - 1
HALO = POOL_STATE + 1
N_EXPERTS = 256
TOP_K = 8
N_EXPERT_GROUPS = 8
TOPK_GROUPS = 4
GROUP_SIZE = N_EXPERTS // N_EXPERT_GROUPS
EXPERT_DIM = D_MODEL // 4
ROUTED_SCALE = 2.5
LN_EPS = 1e-5
DEPTH = 1
DEEPNORM_ALPHA = (2 * DEPTH) ** 0.25

SUBLANES, LANES = 8, 128
ROW_TILE = (SUBLANES, LANES)
assert SUBLANES * LANES == D_MODEL
EXPERT_X_SLOTS = 4
EXPERT_Y_SLOTS = 3
KEY_TILE = 128
MAX_QUERY_TILE = 256
EXPERT_BLOCK = 256
TOKEN_TILE = 512
GATHER_TILE = 256
EXP_UNDERFLOW = 104.0
VMEM_LIMIT = 56 * 1024 * 1024

_NT = (((1,), (1,)), ((), ()))


def _params(*semantics):
    return pltpu.CompilerParams(dimension_semantics=semantics, vmem_limit_bytes=VMEM_LIMIT)


def _layer_norm(x, g, b):
    mu = jnp.mean(x, axis=-1, keepdims=True)
    xc = x - mu
    var = jnp.mean(xc * xc, axis=-1, keepdims=True)
    return xc * lax.rsqrt(var + LN_EPS) * g + b


def _sigmoid(x):
    return 1.0 / (1.0 + jnp.exp(-x))


def _dot(a, b):
    return jnp.dot(a, b, preferred_element_type=F32)


def _to_tiles(x):
    return x.reshape((x.shape[0],) + ROW_TILE)


def _from_tiles(x):
    return x.reshape(x.shape[0], D_MODEL)


def _proj_kernel(x_ref, g_ref, b_ref, w_ref, qb_ref, kb_ref, vb_ref, k_ref, v_ref, u_ref, sg_ref):
    h = _layer_norm(x_ref[0], g_ref[...], b_ref[...]).astype(BF16)

    def proj(lo, hi):
        return _dot(h, w_ref[:, lo:hi])

    qb_ref[0] = (proj(0, SB_WIDTH) * SB_SCALE).astype(BF16)
    zk = proj(SB_WIDTH, 2 * SB_WIDTH)
    kb_ref[0] = zk.astype(BF16)
    zv = proj(2 * SB_WIDTH, 3 * SB_WIDTH)
    vb_ref[0] = zv.astype(BF16)
    for hd in range(SB_HEADS):
        sl = slice(hd * SB_HEAD_DIM, (hd + 1) * SB_HEAD_DIM)
        k_ref[0, 0, hd] = zk[:, sl]
        v_ref[0, 0, hd] = zv[:, sl]
    u0 = 3 * SB_WIDTH
    u_ref[0] = proj(u0, u0 + POOL_WIDTH)
    g0 = u0 + POOL_WIDTH
    for c in range(2 * D_MODEL // SB_WIDTH):
        sl = slice(c * SB_WIDTH, (c + 1) * SB_WIDTH)
        sg_ref[0, :, sl] = _sigmoid(proj(g0 + c * SB_WIDTH, g0 + (c + 1) * SB_WIDTH)).astype(BF16)


def _proj(x, ln_g, ln_b, w_in_b):
    b, t, d = x.shape
    tm = min(TOKEN_TILE, t)
    in_width = w_in_b.shape[1]
    row = lambda width: pl.BlockSpec((1, tm, width), lambda i, j: (i, j, 0))
    heads = pl.BlockSpec((1, 1, SB_HEADS, tm, SB_HEAD_DIM), lambda i, j: (0, i, 0, j, 0))
    const = lambda shape: pl.BlockSpec(shape, lambda i, j: (0,) * len(shape))
    head_shape = jax.ShapeDtypeStruct((1, b, SB_HEADS, t, SB_HEAD_DIM), F32)
    return pl.pallas_call(
        _proj_kernel,
        grid=(b, t // tm),
        in_specs=[row(d), const((1, d)), const((1, d)), const((d, in_width))],
        out_specs=[row(SB_WIDTH), row(SB_WIDTH), row(SB_WIDTH), heads, heads, row(POOL_WIDTH), row(2 * d)],
        out_shape=[
            jax.ShapeDtypeStruct((b, t, SB_WIDTH), BF16),
            jax.ShapeDtypeStruct((b, t, SB_WIDTH), BF16),
            jax.ShapeDtypeStruct((b, t, SB_WIDTH), BF16),
            head_shape,
            head_shape,
            jax.ShapeDtypeStruct((b, t, POOL_WIDTH), F32),
            jax.ShapeDtypeStruct((b, t, 2 * d), BF16),
        ],
        compiler_params=_params("parallel", "parallel"),
        name="proj",
    )(x, ln_g.reshape(1, d), ln_b.reshape(1, d), w_in_b)


def _attn_kernel(q_ref, k_ref, v_ref, tri_ref, o_ref, acc_ref, carry_ref, *, tq, q_pos0):
    tk = KEY_TILE
    q_first = q_pos0 + pl.program_id(1) * tq
    acc_ref[...] = jnp.zeros_like(acc_ref)
    carry_ref[...] = jnp.zeros_like(carry_ref)
    q_pos = q_first + lax.broadcasted_iota(I32, (tq, tk), 0)
    k_off = lax.broadcasted_iota(I32, (tq, tk), 1)
    low_half = lax.broadcasted_iota(I32, (1, 2 * SB_HEAD_DIM), 1) < SB_HEAD_DIM
    tri = tri_ref[...]

    def step(state):
        j, _ = state
        k0 = pl.multiple_of(j * tk, tk)
        mask = (k0 + k_off) < q_pos
        worst = jnp.full((tq, tk), -jnp.inf, F32)
        for pair in range(SB_HEADS // 2):
            lanes = slice(pair * 2 * SB_HEAD_DIM, (pair + 1) * 2 * SB_HEAD_DIM)
            q2 = q_ref[0, :, lanes]
            k2 = k_ref[0, pl.ds(k0, tk), lanes]
            v2 = v_ref[0, pl.ds(k0, tk), lanes]
            out = acc_ref[:, lanes]
            for sub in range(2):
                head = 2 * pair + sub
                own = low_half if sub == 0 else jnp.logical_not(low_half)
                s = lax.dot_general(jnp.where(own, q2, jnp.zeros_like(q2)), k2, _NT,
                                    preferred_element_type=F32)
                soft = jnp.log(1.0 + jnp.exp(-jnp.abs(s)))
                log_beta = jnp.minimum(s, 0.0) - soft
                log_rest = jnp.where(mask, -jnp.maximum(s, 0.0) - soft, 0.0)
                hi = log_rest.astype(BF16)
                lo = (log_rest - hi.astype(F32)).astype(BF16)
                sums = _dot(hi, tri) + _dot(lo, tri)
                carry = carry_ref[head]
                a = jnp.where(mask, jnp.exp(log_beta + sums[:, :tk] + carry), 0.0)
                carry = carry + sums[:, tk:]
                carry_ref[head] = carry
                worst = jnp.maximum(worst, carry)
                out = out + _dot(a.astype(BF16), jnp.where(own, v2, jnp.zeros_like(v2)))
            acc_ref[:, lanes] = out
        return j - 1, (jnp.max(worst) > -EXP_UNDERFLOW).astype(I32)

    j_first = (q_first + tq - 1) // tk
    lax.while_loop(lambda st: jnp.logical_and(st[0] >= 0, st[1] > 0), step, (j_first, jnp.int32(1)))
    o_ref[0] = acc_ref[...].astype(BF16)


def _attn(qb, kb, vb, *, q_pos0):
    b, t, w = qb.shape
    s_k = kb.shape[1]
    tq = min(MAX_QUERY_TILE, t)
    tk = KEY_TILE
    assert t % tq == 0 and s_k % tk == 0 and s_k >= q_pos0 + t
    r = lax.broadcasted_iota(I32, (tk, 2 * tk), 0)
    c = lax.broadcasted_iota(I32, (tk, 2 * tk), 1)
    tri = jnp.logical_or(r > c, c >= tk).astype(BF16)
    return pl.pallas_call(
        functools.partial(_attn_kernel, tq=tq, q_pos0=q_pos0),
        grid=(b, t // tq),
        in_specs=[
            pl.BlockSpec((1, tq, w), lambda i, j: (i, j, 0)),
            pl.BlockSpec((1, s_k, w), lambda i, j: (i, 0, 0)),
            pl.BlockSpec((1, s_k, w), lambda i, j: (i, 0, 0)),
            pl.BlockSpec((tk, 2 * tk), lambda i, j: (0, 0)),
        ],
        out_specs=pl.BlockSpec((1, tq, w), lambda i, j: (i, j, 0)),
        out_shape=jax.ShapeDtypeStruct((b, t, w), BF16),
        scratch_shapes=[pltpu.VMEM((tq, w), F32), pltpu.VMEM((SB_HEADS, tq, tk), F32)],
        compiler_params=_params("parallel", "parallel"),
        name="attn",
    )(qb, kb, vb, tri)


def _merge_kernel(x_ref, osb_ref, u_ref, uprev_ref, hist_ref, sg_ref, g0_ref, b0_ref, wsb_ref, wgrp_ref,
                  pscale_ref, wpool_ref, wo_ref, g1_ref, b1_ref, h_ref, ext_ref, *, tm, hist_valid):
    t = pl.program_id(1)
    h0 = _layer_norm(x_ref[0], g0_ref[...], b0_ref[...])
    u = u_ref[0]
    ext_ref[:HALO, :] = jnp.where(t == 0, hist_ref[0], uprev_ref[0])
    ext_ref[HALO:, :] = u
    rows_seen = t * tm + lax.broadcasted_iota(I32, (tm, 1), 0) + (1 + hist_valid)
    pool_out = []
    for g, window in enumerate(POOL_WINDOWS):
        lanes = slice(g * POOL_GROUP_WIDTH, (g + 1) * POOL_GROUP_WIDTH)
        total = u[:, lanes]
        for back in range(1, window):
            total = total + ext_ref[HALO - back:HALO - back + tm, lanes]
        count = jnp.minimum(rows_seen, window).astype(F32)
        pooled = total / count - u[:, lanes]
        pool_out.append(_dot(pooled.astype(BF16), wgrp_ref[g]))
    o_pool = jnp.concatenate(pool_out, axis=1) * pscale_ref[...]
    a = _dot(osb_ref[0], wsb_ref[...])
    p = _dot(o_pool.astype(BF16), wpool_ref[...])
    sg = sg_ref[0].astype(F32)
    m = sg[:, :D_MODEL] * a + sg[:, D_MODEL:] * p
    mix = _dot(m.astype(BF16), wo_ref[...])
    h1 = _layer_norm(DEEPNORM_ALPHA * h0 + mix, g1_ref[...], b1_ref[...])
    h_ref[...] = _to_tiles(h1)


def _merge(x, osb, u, hist, sg, ln0_g, ln0_b, wsb, wgrp, pscale, wpool, wo, ln1_g, ln1_b, *, hist_valid):
    b, t, d = x.shape
    tm = min(TOKEN_TILE, t)
    nt = t // tm
    row = lambda width: pl.BlockSpec((1, tm, width), lambda i, j: (i, j, 0))
    const = lambda shape: pl.BlockSpec(shape, lambda i, j: (0,) * len(shape))
    prev = pl.BlockSpec((1, HALO, POOL_WIDTH), lambda i, j: (i, jnp.maximum(j * (tm // HALO) - 1, 0), 0))
    return pl.pallas_call(
        functools.partial(_merge_kernel, tm=tm, hist_valid=hist_valid),
        grid=(b, nt),
        in_specs=[
            row(d), row(SB_WIDTH), row(POOL_WIDTH), prev,
            pl.BlockSpec((1, HALO, POOL_WIDTH), lambda i, j: (i, 0, 0)),
            row(2 * d), const((1, d)), const((1, d)), const((SB_WIDTH, d)),
            const((len(POOL_WINDOWS), POOL_GROUP_WIDTH, POOL_GROUP_WIDTH)), const((1, POOL_WIDTH)),
            const((POOL_WIDTH, d)), const((d, d)), const((1, d)), const((1, d)),
        ],
        out_specs=pl.BlockSpec((tm,) + ROW_TILE, lambda i, j: (i * nt + j, 0, 0)),
        out_shape=jax.ShapeDtypeStruct((b * t,) + ROW_TILE, F32),
        scratch_shapes=[pltpu.VMEM((HALO + tm, POOL_WIDTH), F32)],
        compiler_params=_params("parallel", "parallel"),
        name="merge",
    )(x, osb, u, u, hist, sg, ln0_g.reshape(1, d), ln0_b.reshape(1, d), wsb, wgrp, pscale.reshape(1, -1),
      wpool, wo, ln1_g.reshape(1, d), ln1_b.reshape(1, d))


def _first_max(vals, index, n):
    top = jnp.max(vals, axis=0, keepdims=True)
    first = jnp.min(jnp.where(vals == top, index, n), axis=0, keepdims=True)
    return top, first


def _route_kernel(h_ref, wr_ref, bias_ref, tri_ref, cin_ref, idx_ref, gate_ref, rank_ref, cout_ref, *, tm):
    @pl.when(pl.program_id(0) == 0)
    def _():
        cout_ref[...] = cin_ref[...]

    h = _from_tiles(h_ref[...]).astype(BF16)
    scores = _sigmoid(lax.dot_general(wr_ref[...], h, _NT, preferred_element_type=F32))
    biased = scores + bias_ref[...]
    neg = jnp.float32(-jnp.inf)

    in_group = lax.broadcasted_iota(I32, (GROUP_SIZE, tm), 0)
    group_scores = []
    for g in range(N_EXPERT_GROUPS):
        blk = biased[g * GROUP_SIZE:(g + 1) * GROUP_SIZE, :]
        top, first = _first_max(blk, in_group, GROUP_SIZE)
        second = jnp.max(jnp.where(in_group == first, neg, blk), axis=0, keepdims=True)
        group_scores.append(top + second)
    remaining = jnp.concatenate(group_scores, axis=0)
    group_id = lax.broadcasted_iota(I32, (N_EXPERT_GROUPS, tm), 0)
    kept = jnp.zeros((N_EXPERT_GROUPS, tm), F32)
    for _ in range(TOPK_GROUPS):
        _, first = _first_max(remaining, group_id, N_EXPERT_GROUPS)
        hit = group_id == first
        kept = jnp.where(hit, 1.0, kept)
        remaining = jnp.where(hit, neg, remaining)
    remaining = jnp.concatenate(
        [jnp.where(kept[g:g + 1, :] > 0.5, biased[g * GROUP_SIZE:(g + 1) * GROUP_SIZE, :], neg)
         for g in range(N_EXPERT_GROUPS)], axis=0)

    expert_id = lax.broadcasted_iota(I32, (N_EXPERTS, tm), 0)
    chosen, picks, picked_scores = jnp.zeros((N_EXPERTS, tm), F32), [], []
    for _ in range(TOP_K):
        _, first = _first_max(remaining, expert_id, N_EXPERTS)
        hit = expert_id == first
        picks.append((first, hit))
        picked_scores.append(jnp.sum(jnp.where(hit, scores, 0.0), axis=0, keepdims=True))
        chosen = jnp.where(hit, 1.0, chosen)
        remaining = jnp.where(hit, neg, remaining)
    denom = picked_scores[0]
    for sc in picked_scores[1:]:
        denom = denom + sc
    gate_ref[...] = jnp.concatenate([sc / denom * ROUTED_SCALE for sc in picked_scores], axis=0)
    idx_ref[...] = jnp.concatenate([first for first, _ in picks], axis=0)

    slot = cout_ref[...] + _dot(chosen.astype(BF16), tri_ref[...])
    rank_ref[...] = jnp.concatenate(
        [jnp.sum(jnp.where(hit, slot, 0.0), axis=0, keepdims=True) for _, hit in picks], axis=0).astype(I32)
    cout_ref[...] += jnp.sum(chosen, axis=1, keepdims=True)


def _route(h, wr_t, bias, count_in):
    n = h.shape[0]
    tm = min(TOKEN_TILE, n)
    assert n % tm == 0
    earlier = (lax.broadcasted_iota(I32, (tm, tm), 0) < lax.broadcasted_iota(I32, (tm, tm), 1)).astype(BF16)
    const = lambda shape: pl.BlockSpec(shape, lambda i: (0,) * len(shape))
    picks = pl.BlockSpec((TOP_K, tm), lambda i: (0, i))
    return pl.pallas_call(
        functools.partial(_route_kernel, tm=tm),
        grid=(n // tm,),
        in_specs=[pl.BlockSpec((tm,) + ROW_TILE, lambda i: (i, 0, 0)), const((N_EXPERTS, D_MODEL)),
                  const((N_EXPERTS, 1)), const((tm, tm)), const((N_EXPERTS, 1))],
        out_specs=[picks, picks, picks, const((N_EXPERTS, 1))],
        out_shape=[jax.ShapeDtypeStruct((TOP_K, n), I32), jax.ShapeDtypeStruct((TOP_K, n), F32),
                   jax.ShapeDtypeStruct((TOP_K, n), I32), jax.ShapeDtypeStruct((N_EXPERTS, 1), F32)],
        compiler_params=_params("arbitrary"),
        name="route",
    )(h, wr_t, bias, earlier, count_in)


def _slot_kernel(idx_ref, rank_ref, start_ref, dest_ref, *, tm):
    expert_id = lax.broadcasted_iota(I32, (N_EXPERTS, tm), 0)
    start = start_ref[...]
    rows = [jnp.sum(jnp.where(expert_id == idx_ref[j:j + 1, :], start, 0.0), axis=0, keepdims=True)
            for j in range(TOP_K)]
    dest_ref[...] = jnp.concatenate(rows, axis=0).astype(I32) + rank_ref[...]


def _slots(idx, rank, start):
    n = idx.shape[1]
    tm = min(TOKEN_TILE, n)
    assert n % tm == 0
    picks = pl.BlockSpec((TOP_K, tm), lambda i: (0, i))
    return pl.pallas_call(
        functools.partial(_slot_kernel, tm=tm),
        grid=(n // tm,),
        in_specs=[picks, picks, pl.BlockSpec((N_EXPERTS, 1), lambda i: (0, 0))],
        out_specs=picks,
        out_shape=jax.ShapeDtypeStruct((TOP_K, n), I32),
        compiler_params=_params("parallel"),
        name="slots",
    )(idx, rank, start)


def _start_row_copies(tm, copy_of):
    def group(g, carry):
        tok0 = pl.multiple_of(g * SUBLANES, SUBLANES)
        for r in range(SUBLANES):
            for j in range(TOP_K):
                copy_of(tok0 + r, j, tok0 * TOP_K + (r * TOP_K + j)).start(priority=(r + j) % 2)
        return carry

    lax.fori_loop(0, tm // SUBLANES, group, 0)


def _dispatch_kernel(dest_ref, h_ref, *rest, tm):
    xs_ref, sem = rest[-2], rest[-1]
    _start_row_copies(
        tm, lambda tok, j, pair: pltpu.make_async_copy(h_ref.at[tok], xs_ref.at[dest_ref[pair]], sem))
    for _ in range(TOP_K):
        pltpu.make_async_copy(h_ref, xs_ref.at[pl.ds(0, tm)], sem).wait()


def _dispatch(dest, h, xs, cap):
    n = h.shape[0]
    tm = min(GATHER_TILE, n)
    assert n % tm == 0
    in_specs = [pl.BlockSpec((tm * TOP_K,), lambda i: (i,), memory_space=pltpu.SMEM),
                pl.BlockSpec((tm,) + ROW_TILE, lambda i: (i, 0, 0))]
    args = [dest, h]
    aliases = {}
    if xs is not None:
        in_specs.append(pl.BlockSpec(memory_space=pl.ANY))
        args.append(xs)
        aliases = {2: 0}
    return pl.pallas_call(
        functools.partial(_dispatch_kernel, tm=tm),
        grid=(n // tm,),
        in_specs=in_specs,
        out_specs=pl.BlockSpec(memory_space=pl.ANY),
        out_shape=jax.ShapeDtypeStruct((cap,) + ROW_TILE, F32),
        scratch_shapes=[pltpu.SemaphoreType.DMA],
        input_output_aliases=aliases,
        compiler_params=pltpu.CompilerParams(dimension_semantics=("arbitrary",), vmem_limit_bytes=VMEM_LIMIT,
                                             has_side_effects=True),
        name="dispatch",
    )(*args)


def _expert_kernel(first_ref, nblk_ref, used_ref, wg_ref, wu_ref, wd_ref, xs_ref, ys_ref,
                   wg_b, wu_b, wd_b, x_buf, y_buf, x_sem, y_sem):
    e = pl.program_id(0)
    blk, nx, ny = EXPERT_BLOCK, EXPERT_X_SLOTS, EXPERT_Y_SLOTS
    first, n, used = first_ref[e], nblk_ref[e], used_ref[0]

    def x_copy(g):
        slot = g % nx
        return pltpu.make_async_copy(xs_ref.at[pl.ds(pl.multiple_of(g * blk, blk), blk)], x_buf.at[slot],
                                     x_sem.at[slot])

    def y_copy(g):
        slot = g % ny
        return pltpu.make_async_copy(y_buf.at[slot], ys_ref.at[pl.ds(pl.multiple_of(g * blk, blk), blk)],
                                     y_sem.at[slot])

    @pl.when(e == 0)
    def _():
        for g in range(nx - 1):
            @pl.when(g < used)
            def _():
                x_copy(g).start()

    @pl.when(n > 0)
    def _():
        wg_b[...] = wg_ref[0].astype(BF16)
        wu_b[...] = wu_ref[0].astype(BF16)
        wd_b[...] = wd_ref[0].astype(BF16)

        def block(b, carry):
            g = first + b

            @pl.when(g + (nx - 1) < used)
            def _():
                x_copy(g + (nx - 1)).start()

            x_copy(g).wait()
            x = _from_tiles(x_buf[g % nx]).astype(BF16)
            gate = _dot(x, wg_b[...])
            hid = gate * _sigmoid(gate) * _dot(x, wu_b[...])
            y = _dot(hid.astype(BF16), wd_b[...])

            @pl.when(g >= ny)
            def _():
                y_copy(g - ny).wait()

            y_buf[g % ny] = _to_tiles(y)
            y_copy(g).start()
            return carry

        lax.fori_loop(0, n, block, 0)

    @pl.when(e == pl.num_programs(0) - 1)
    def _():
        for back in range(ny, 0, -1):
            @pl.when(used >= back)
            def _():
                y_copy(used - back).wait()


def _experts(first, nblk, used, xs, w_gate, w_up, w_down):
    cap = xs.shape[0]
    blk = EXPERT_BLOCK
    gate_up = pl.BlockSpec((1, D_MODEL, EXPERT_DIM), lambda e, *_: (e, 0, 0))
    return pl.pallas_call(
        _expert_kernel,
        grid_spec=pltpu.PrefetchScalarGridSpec(
            num_scalar_prefetch=3,
            grid=(N_EXPERTS,),
            in_specs=[gate_up, gate_up, pl.BlockSpec((1, EXPERT_DIM, D_MODEL), lambda e, *_: (e, 0, 0)),
                      pl.BlockSpec(memory_space=pl.ANY)],
            out_specs=pl.BlockSpec(memory_space=pl.ANY),
            scratch_shapes=[pltpu.VMEM((D_MODEL, EXPERT_DIM), BF16), pltpu.VMEM((D_MODEL, EXPERT_DIM), BF16),
                            pltpu.VMEM((EXPERT_DIM, D_MODEL), BF16),
                            pltpu.VMEM((EXPERT_X_SLOTS, blk) + ROW_TILE, F32),
                            pltpu.VMEM((EXPERT_Y_SLOTS, blk) + ROW_TILE, F32),
                            pltpu.SemaphoreType.DMA((EXPERT_X_SLOTS,)), pltpu.SemaphoreType.DMA((EXPERT_Y_SLOTS,))],
        ),
        out_shape=jax.ShapeDtypeStruct((cap,) + ROW_TILE, F32),
        compiler_params=_params("arbitrary"),
        name="experts",
    )(first, nblk, used, w_gate, w_up, w_down, xs)


def _combine_kernel(dest_ref, h_ref, gate_ref, ys_ref, wsg_ref, wsu_ref, wsd_ref, g_ref, b_ref, o_ref,
                    rows_ref, sem, *, tm):
    _start_row_copies(
        tm, lambda tok, j, pair: pltpu.make_async_copy(ys_ref.at[dest_ref[pair]], rows_ref.at[j, tok], sem))
    h = _from_tiles(h_ref[...])
    hb = h.astype(BF16)
    gate = _dot(hb, wsg_ref[...])
    hid = gate * _sigmoid(gate) * _dot(hb, wsu_ref[...])
    ffn = _dot(hid.astype(BF16), wsd_ref[...])
    for j in range(TOP_K):
        pltpu.make_async_copy(ys_ref.at[pl.ds(0, tm)], rows_ref.at[j], sem).wait()
    weights = gate_ref[...]
    for j in range(TOP_K):
        ffn = ffn + weights[:, j:j + 1] * _from_tiles(rows_ref[j])
    o_ref[...] = _layer_norm(DEEPNORM_ALPHA * h + ffn, g_ref[...], b_ref[...])


def _combine(dest, h, gate_t, ys, wsg, wsu, wsd, ln_g, ln_b):
    n, d = h.shape[0], D_MODEL
    tm = min(GATHER_TILE, n)
    assert n % tm == 0
    const = lambda shape: pl.BlockSpec(shape, lambda i: (0,) * len(shape))
    return pl.pallas_call(
        functools.partial(_combine_kernel, tm=tm),
        grid=(n // tm,),
        in_specs=[pl.BlockSpec((tm * TOP_K,), lambda i: (i,), memory_space=pltpu.SMEM),
                  pl.BlockSpec((tm,) + ROW_TILE, lambda i: (i, 0, 0)),
                  pl.BlockSpec((tm, TOP_K), lambda i: (i, 0)),
                  pl.BlockSpec(memory_space=pl.ANY),
                  const((d, EXPERT_DIM)), const((d, EXPERT_DIM)), const((EXPERT_DIM, d)),
                  const((1, d)), const((1, d))],
        out_specs=pl.BlockSpec((tm, d), lambda i: (i, 0)),
        out_shape=jax.ShapeDtypeStruct((n, d), F32),
        scratch_shapes=[pltpu.VMEM((TOP_K, tm) + ROW_TILE, F32), pltpu.SemaphoreType.DMA],
        compiler_params=_params("arbitrary"),
        name="combine",
    )(dest, h, gate_t, ys, wsg, wsu, wsd, ln_g.reshape(1, d), ln_b.reshape(1, d))


def _key_major(cache):
    b, h, s, hd = cache.shape
    return cache.transpose(0, 2, 1, 3).reshape(b, s, h * hd).astype(BF16)


def kernel(x_prompt, x_sample, cache_k, cache_v, state_pool, ln0_g, ln0_b, w_in, w_sb_up, w_pool_grp, pool_scale,
           w_pool_up, w_o, ln1_g, ln1_b, w_router, router_bias, w_exp_gate, w_exp_up, w_exp_down, w_sh_gate,
           w_sh_up, w_sh_down, ln2_g, ln2_b):
    assert w_in.shape[0] == DEPTH
    b_p, t_p, d = x_prompt.shape
    b_s, t_s, _ = x_sample.shape
    past = cache_k.shape[3]
    n_p, n_s = b_p * t_p, b_s * t_s
    layer = 0
    w_in_b = w_in[layer].astype(BF16)
    mix_w = (w_sb_up[layer].astype(BF16), w_pool_grp[layer].astype(BF16), pool_scale[layer],
             w_pool_up[layer].astype(BF16), w_o[layer].astype(BF16), ln1_g[layer], ln1_b[layer])

    qb, kb, vb, k_p, v_p, u_p, sg_p = _proj(x_prompt, ln0_g, ln0_b, w_in_b)
    osb_p = _attn(qb, kb, vb, q_pos0=0)
    hist_p = jnp.zeros((b_p, HALO, POOL_WIDTH), F32)
    h_p = _merge(x_prompt, osb_p, u_p, hist_p, sg_p, ln0_g, ln0_b, *mix_w, hist_valid=0)

    qs, ks, vs, k_s, v_s, u_s, sg_s = _proj(x_sample, ln0_g, ln0_b, w_in_b)
    key_rows = -(-(past + t_s) // KEY_TILE) * KEY_TILE
    pad = ((0, 0), (0, key_rows - past - t_s), (0, 0))
    k_all = jnp.pad(jnp.concatenate([_key_major(cache_k[layer]), ks], axis=1), pad)
    v_all = jnp.pad(jnp.concatenate([_key_major(cache_v[layer]), vs], axis=1), pad)
    osb_s = _attn(qs, k_all, v_all, q_pos0=past)
    hist_s = jnp.concatenate([jnp.zeros((b_s, 1, POOL_WIDTH), F32), state_pool[layer]], axis=1)
    h_s = _merge(x_sample, osb_s, u_s, hist_s, sg_s, ln0_g, ln0_b, *mix_w, hist_valid=POOL_STATE)

    wr_t = w_router[layer].T.astype(BF16)
    bias = router_bias[layer].astype(F32).reshape(N_EXPERTS, 1)
    idx_p, gate_p, rank_p, count = _route(h_p, wr_t, bias, jnp.zeros((N_EXPERTS, 1), F32))
    idx_s, gate_s, rank_s, count = _route(h_s, wr_t, bias, count)

    blk = EXPERT_BLOCK
    n_blocks = ((n_p + n_s) * TOP_K + N_EXPERTS * (blk - 1) + blk - 1) // blk
    cap = n_blocks * blk
    assert cap < 2 ** 24
    nblk = (count[:, 0].astype(I32) + blk - 1) // blk
    first = jnp.cumsum(nblk) - nblk
    used = jnp.sum(nblk).reshape(1)
    start = (first * blk).astype(F32).reshape(N_EXPERTS, 1)
    dest_p = _slots(idx_p, rank_p, start).T.reshape(-1)
    dest_s = _slots(idx_s, rank_s, start).T.reshape(-1)

    xs = _dispatch(dest_p, h_p, None, cap)
    xs = _dispatch(dest_s, h_s, xs, cap)
    ys = _experts(first, nblk, used, xs, w_exp_gate[layer], w_exp_up[layer], w_exp_down[layer])

    shared_w = (w_sh_gate[layer].astype(BF16), w_sh_up[layer].astype(BF16), w_sh_down[layer].astype(BF16),
                ln2_g[layer], ln2_b[layer])
    y_p = _combine(dest_p, h_p, gate_p.T, ys, *shared_w)
    y_s = _combine(dest_s, h_s, gate_s.T, ys, *shared_w)

    new_pool_p = u_p[:, t_p - POOL_STATE:][None]
    new_pool_s = jnp.concatenate([state_pool[layer], u_s], axis=1)[:, -POOL_STATE:][None]
    return (y_p.reshape(b_p, t_p, d), y_s.reshape(b_s, t_s, d), k_p, v_p, new_pool_p, k_s, v_s, new_pool_s)
```

```python
import functools

import jax
import jax.numpy as jnp
from jax import lax
from jax.experimental import pallas as pl
from jax.experimental.pallas import tpu as pltpu

F32 = jnp.float32
BF16 = jnp.bfloat16
I32 = jnp.int32
U32 = jnp.uint32

D_MODEL = 1024
SB_HEADS = 8
SB_HEAD_DIM = 64
SB_WIDTH = SB_HEADS * SB_HEAD_DIM
SB_SCALE = SB_HEAD_DIM ** -0.5
POOL_WINDOWS = (2, 4, 8, 16)
POOL_WIDTH = D_MODEL // 2
POOL_GROUP_WIDTH = POOL_WIDTH // len(POOL_WINDOWS)
POOL_STATE = max(POOL_WINDOWS) - 1
HALO = POOL_STATE + 1
N_EXPERTS = 256
TOP_K = 8
N_EXPERT_GROUPS = 8
TOPK_GROUPS = 4
GROUP_SIZE = N_EXPERTS // N_EXPERT_GROUPS
EXPERT_DIM = D_MODEL // 4
ROUTED_SCALE = 2.5
LN_EPS = 1e-5
DEPTH = 1
DEEPNORM_ALPHA = (2 * DEPTH) ** 0.25

SUBLANES, LANES = 8, 128
ROW_TILE = (SUBLANES, LANES)
assert SUBLANES * LANES == D_MODEL
EXPERT_X_SLOTS = 4
EXPERT_Y_SLOTS = 3
KEY_TILE = 128
MAX_QUERY_TILE = 128
EXPERT_BLOCK = 256
TOKEN_TILE = 512
GATHER_TILE = 256
EXP_UNDERFLOW = 104.0
VMEM_LIMIT = 56 * 1024 * 1024

_NT = (((1,), (1,)), ((), ()))


def _params(*semantics):
    return pltpu.CompilerParams(dimension_semantics=semantics, vmem_limit_bytes=VMEM_LIMIT)


def _layer_norm(x, g, b):
    mu = jnp.mean(x, axis=-1, keepdims=True)
    xc = x - mu
    var = jnp.mean(xc * xc, axis=-1, keepdims=True)
    return xc * lax.rsqrt(var + LN_EPS) * g + b


def _sigmoid(x):
    return 1.0 / (1.0 + jnp.exp(-x))


def _dot(a, b):
    return jnp.dot(a, b, preferred_element_type=F32)


def _to_tiles(x):
    return x.reshape((x.shape[0],) + ROW_TILE)


def _from_tiles(x):
    return x.reshape(x.shape[0], D_MODEL)


def _proj_kernel(x_ref, g_ref, b_ref, w_ref, qb_ref, kb_ref, vb_ref, k_ref, v_ref, u_ref, sg_ref):
    h = _layer_norm(x_ref[0], g_ref[...], b_ref[...]).astype(BF16)

    def proj(lo, hi):
        return _dot(h, w_ref[:, lo:hi])

    qb_ref[0] = (proj(0, SB_WIDTH) * SB_SCALE).astype(BF16)
    zk = proj(SB_WIDTH, 2 * SB_WIDTH)
    kb_ref[0] = zk.astype(BF16)
    zv = proj(2 * SB_WIDTH, 3 * SB_WIDTH)
    vb_ref[0] = zv.astype(BF16)
    for hd in range(SB_HEADS):
        sl = slice(hd * SB_HEAD_DIM, (hd + 1) * SB_HEAD_DIM)
        k_ref[0, 0, hd] = zk[:, sl]
        v_ref[0, 0, hd] = zv[:, sl]
    u0 = 3 * SB_WIDTH
    u_ref[0] = proj(u0, u0 + POOL_WIDTH)
    g0 = u0 + POOL_WIDTH
    for c in range(2 * D_MODEL // SB_WIDTH):
        sl = slice(c * SB_WIDTH, (c + 1) * SB_WIDTH)
        sg_ref[0, :, sl] = _sigmoid(proj(g0 + c * SB_WIDTH, g0 + (c + 1) * SB_WIDTH)).astype(BF16)


def _proj(x, ln_g, ln_b, w_in_b):
    b, t, d = x.shape
    tm = min(TOKEN_TILE, t)
    in_width = w_in_b.shape[1]
    row = lambda width: pl.BlockSpec((1, tm, width), lambda i, j: (i, j, 0))
    heads = pl.BlockSpec((1, 1, SB_HEADS, tm, SB_HEAD_DIM), lambda i, j: (0, i, 0, j, 0))
    const = lambda shape: pl.BlockSpec(shape, lambda i, j: (0,) * len(shape))
    head_shape = jax.ShapeDtypeStruct((1, b, SB_HEADS, t, SB_HEAD_DIM), F32)
    return pl.pallas_call(
        _proj_kernel,
        grid=(b, t // tm),
        in_specs=[row(d), const((1, d)), const((1, d)), const((d, in_width))],
        out_specs=[row(SB_WIDTH), row(SB_WIDTH), row(SB_WIDTH), heads, heads, row(POOL_WIDTH), row(2 * d)],
        out_shape=[
            jax.ShapeDtypeStruct((b, t, SB_WIDTH), BF16),
            jax.ShapeDtypeStruct((b, t, SB_WIDTH), BF16),
            jax.ShapeDtypeStruct((b, t, SB_WIDTH), BF16),
            head_shape,
            head_shape,
            jax.ShapeDtypeStruct((b, t, POOL_WIDTH), F32),
            jax.ShapeDtypeStruct((b, t, 2 * d), BF16),
        ],
        compiler_params=_params("parallel", "parallel"),
        name="proj",
    )(x, ln_g.reshape(1, d), ln_b.reshape(1, d), w_in_b)


def _attn_kernel(q_ref, k_ref, v_ref, tri_ref, o_ref, acc_ref, carry_ref, qm_ref, *, tq, q_pos0):
    tk = KEY_TILE
    pairs = SB_HEADS // 2
    q_first = q_pos0 + pl.program_id(1) * tq
    low_half = lax.broadcasted_iota(I32, (1, 2 * SB_HEAD_DIM), 1) < SB_HEAD_DIM

    def split_heads(x2):
        zero = jnp.zeros_like(x2)
        return jnp.concatenate([jnp.where(low_half, x2, zero), jnp.where(low_half, zero, x2)], axis=0)

    acc_ref[...] = jnp.zeros_like(acc_ref)
    carry_ref[...] = jnp.zeros_like(carry_ref)
    for pair in range(pairs):
        qm_ref[pair] = split_heads(q_ref[0, :, pair * 2 * SB_HEAD_DIM:(pair + 1) * 2 * SB_HEAD_DIM])

    def step(j, masked):
        k0 = pl.multiple_of(j * tk, tk)
        k_tile = k_ref[0, pl.ds(k0, tk), :]
        v_tile = v_ref[0, pl.ds(k0, tk), :]
        s = jnp.concatenate(
            [lax.dot_general(qm_ref[pair], k_tile[:, pair * 2 * SB_HEAD_DIM:(pair + 1) * 2 * SB_HEAD_DIM], _NT,
                             preferred_element_type=F32) for pair in range(pairs)],
            axis=0).reshape(SB_HEADS, tq, tk)
        soft = jnp.log(1.0 + jnp.exp(-jnp.abs(s)))
        log_rest = -(jnp.maximum(s, 0.0) + soft)
        log_beta = log_rest + s
        if masked:
            mask = ((k0 + lax.broadcasted_iota(I32, (tq, tk), 1))
                    < (q_first + lax.broadcasted_iota(I32, (tq, tk), 0)))[None]
            log_rest = jnp.where(mask, log_rest, 0.0)
        hi = log_rest.astype(BF16)
        lo = (log_rest - hi.astype(F32)).astype(BF16)
        sums = _dot(jnp.concatenate([hi, lo], axis=-1).reshape(SB_HEADS * tq, 2 * tk), tri_ref[...])
        sums = sums.reshape(SB_HEADS, tq, 2 * tk)
        carry = carry_ref[...]
        a = jnp.exp(log_beta + sums[:, :, :tk] + carry)
        if masked:
            a = jnp.where(mask, a, 0.0)
        a = a.astype(BF16)
        carry = carry + sums[:, :, tk:]
        carry_ref[...] = carry
        for pair in range(pairs):
            lanes = slice(pair * 2 * SB_HEAD_DIM, (pair + 1) * 2 * SB_HEAD_DIM)
            a2 = jnp.concatenate([a[2 * pair], a[2 * pair + 1]], axis=1)
            acc_ref[:, lanes] += _dot(a2, split_heads(v_tile[:, lanes]))
        return (jnp.max(carry) > -EXP_UNDERFLOW).astype(I32)

    j_first = (q_first + tq - 1) // tk
    go = step(j_first, True)
    lax.while_loop(lambda st: jnp.logical_and(st[0] >= 0, st[1] > 0),
                   lambda st: (st[0] - 1, step(st[0], False)), (j_first - 1, go))
    o_ref[0] = acc_ref[...].astype(BF16)


def _attn(qb, kb, vb, *, q_pos0):
    b, t, w = qb.shape
    s_k = kb.shape[1]
    tq = min(MAX_QUERY_TILE, t)
    tk = KEY_TILE
    assert t % tq == 0 and s_k % tk == 0 and s_k >= q_pos0 + t
    assert tk % tq == 0 and q_pos0 % tq == 0
    r = lax.broadcasted_iota(I32, (2 * tk, 2 * tk), 0) % tk
    c = lax.broadcasted_iota(I32, (2 * tk, 2 * tk), 1)
    tri = jnp.logical_or(r > c, c >= tk).astype(BF16)
    return pl.pallas_call(
        functools.partial(_attn_kernel, tq=tq, q_pos0=q_pos0),
        grid=(b, t // tq),
        in_specs=[
            pl.BlockSpec((1, tq, w), lambda i, j: (i, j, 0)),
            pl.BlockSpec((1, s_k, w), lambda i, j: (i, 0, 0)),
            pl.BlockSpec((1, s_k, w), lambda i, j: (i, 0, 0)),
            pl.BlockSpec((2 * tk, 2 * tk), lambda i, j: (0, 0)),
        ],
        out_specs=pl.BlockSpec((1, tq, w), lambda i, j: (i, j, 0)),
        out_shape=jax.ShapeDtypeStruct((b, t, w), BF16),
        scratch_shapes=[pltpu.VMEM((tq, w), F32), pltpu.VMEM((SB_HEADS, tq, tk), F32),
                        pltpu.VMEM((SB_HEADS // 2, 2 * tq, 2 * SB_HEAD_DIM), BF16)],
        compiler_params=_params("parallel", "parallel"),
        name="attn",
    )(qb, kb, vb, tri)


def _merge_kernel(x_ref, osb_ref, u_ref, uprev_ref, hist_ref, sg_ref, g0_ref, b0_ref, wsb_ref, wgrp_ref,
                  pscale_ref, wpool_ref, wo_ref, g1_ref, b1_ref, h_ref, ext_ref, *, tm, hist_valid):
    t = pl.program_id(1)
    h0 = _layer_norm(x_ref[0], g0_ref[...], b0_ref[...])
    u = u_ref[0]
    ext_ref[:HALO, :] = jnp.where(t == 0, hist_ref[0], uprev_ref[0])
    ext_ref[HALO:, :] = u
    rows_seen = t * tm + lax.broadcasted_iota(I32, (tm, 1), 0) + (1 + hist_valid)
    pool_out = []
    for g, window in enumerate(POOL_WINDOWS):
        lanes = slice(g * POOL_GROUP_WIDTH, (g + 1) * POOL_GROUP_WIDTH)
        total = u[:, lanes]
        for back in range(1, window):
            total = total + ext_ref[HALO - back:HALO - back + tm, lanes]
        count = jnp.minimum(rows_seen, window).astype(F32)
        pooled = total / count - u[:, lanes]
        pool_out.append(_dot(pooled.astype(BF16), wgrp_ref[g]))
    o_pool = jnp.concatenate(pool_out, axis=1) * pscale_ref[...]
    a = _dot(osb_ref[0], wsb_ref[...])
    p = _dot(o_pool.astype(BF16), wpool_ref[...])
    sg = sg_ref[0].astype(F32)
    m = sg[:, :D_MODEL] * a + sg[:, D_MODEL:] * p
    mix = _dot(m.astype(BF16), wo_ref[...])
    h1 = _layer_norm(DEEPNORM_ALPHA * h0 + mix, g1_ref[...], b1_ref[...])
    h_ref[...] = _to_tiles(h1)


def _merge(x, osb, u, hist, sg, ln0_g, ln0_b, wsb, wgrp, pscale, wpool, wo, ln1_g, ln1_b, *, hist_valid):
    b, t, d = x.shape
    tm = min(TOKEN_TILE, t)
    nt = t // tm
    row = lambda width: pl.BlockSpec((1, tm, width), lambda i, j: (i, j, 0))
    const = lambda shape: pl.BlockSpec(shape, lambda i, j: (0,) * len(shape))
    prev = pl.BlockSpec((1, HALO, POOL_WIDTH), lambda i, j: (i, jnp.maximum(j * (tm // HALO) - 1, 0), 0))
    return pl.pallas_call(
        functools.partial(_merge_kernel, tm=tm, hist_valid=hist_valid),
        grid=(b, nt),
        in_specs=[
            row(d), row(SB_WIDTH), row(POOL_WIDTH), prev,
            pl.BlockSpec((1, HALO, POOL_WIDTH), lambda i, j: (i, 0, 0)),
            row(2 * d), const((1, d)), const((1, d)), const((SB_WIDTH, d)),
            const((len(POOL_WINDOWS), POOL_GROUP_WIDTH, POOL_GROUP_WIDTH)), const((1, POOL_WIDTH)),
            const((POOL_WIDTH, d)), const((d, d)), const((1, d)), const((1, d)),
        ],
        out_specs=pl.BlockSpec((tm,) + ROW_TILE, lambda i, j: (i * nt + j, 0, 0)),
        out_shape=jax.ShapeDtypeStruct((b * t,) + ROW_TILE, F32),
        scratch_shapes=[pltpu.VMEM((HALO + tm, POOL_WIDTH), F32)],
        compiler_params=_params("parallel", "parallel"),
        name="merge",
    )(x, osb, u, u, hist, sg, ln0_g.reshape(1, d), ln0_b.reshape(1, d), wsb, wgrp, pscale.reshape(1, -1),
      wpool, wo, ln1_g.reshape(1, d), ln1_b.reshape(1, d))


def _first_max(vals, index, n):
    top = jnp.max(vals, axis=0, keepdims=True)
    first = jnp.min(jnp.where(vals == top, index, n), axis=0, keepdims=True)
    return top, first


def _route_kernel(h_ref, wr_ref, bias_ref, tri_ref, cin_ref, idx_ref, gate_ref, rank_ref, cout_ref, *, tm):
    @pl.when(pl.program_id(0) == 0)
    def _():
        cout_ref[...] = cin_ref[...]

    h = _from_tiles(h_ref[...]).astype(BF16)
    scores = _sigmoid(lax.dot_general(wr_ref[...], h, _NT, preferred_element_type=F32))
    biased = scores + bias_ref[...]
    neg = jnp.float32(-jnp.inf)

    in_group = lax.broadcasted_iota(I32, (GROUP_SIZE, tm), 0)
    group_scores = []
    for g in range(N_EXPERT_GROUPS):
        blk = biased[g * GROUP_SIZE:(g + 1) * GROUP_SIZE, :]
        top, first = _first_max(blk, in_group, GROUP_SIZE)
        second = jnp.max(jnp.where(in_group == first, neg, blk), axis=0, keepdims=True)
        group_scores.append(top + second)
    remaining = jnp.concatenate(group_scores, axis=0)
    group_id = lax.broadcasted_iota(I32, (N_EXPERT_GROUPS, tm), 0)
    kept = jnp.zeros((N_EXPERT_GROUPS, tm), F32)
    for _ in range(TOPK_GROUPS):
        _, first = _first_max(remaining, group_id, N_EXPERT_GROUPS)
        hit = group_id == first
        kept = jnp.where(hit, 1.0, kept)
        remaining = jnp.where(hit, neg, remaining)
    remaining = jnp.concatenate(
        [jnp.where(kept[g:g + 1, :] > 0.5, biased[g * GROUP_SIZE:(g + 1) * GROUP_SIZE, :], neg)
         for g in range(N_EXPERT_GROUPS)], axis=0)

    expert_id = lax.broadcasted_iota(I32, (N_EXPERTS, tm), 0)
    chosen, picks, picked_scores = jnp.zeros((N_EXPERTS, tm), F32), [], []
    for _ in range(TOP_K):
        _, first = _first_max(remaining, expert_id, N_EXPERTS)
        hit = expert_id == first
        picks.append((first, hit))
        picked_scores.append(jnp.sum(jnp.where(hit, scores, 0.0), axis=0, keepdims=True))
        chosen = jnp.where(hit, 1.0, chosen)
        remaining = jnp.where(hit, neg, remaining)
    denom = picked_scores[0]
    for sc in picked_scores[1:]:
        denom = denom + sc
    gate_ref[...] = jnp.concatenate([sc / denom * ROUTED_SCALE for sc in picked_scores], axis=0)
    idx_ref[...] = jnp.concatenate([first for first, _ in picks], axis=0)

    slot = cout_ref[...] + _dot(chosen.astype(BF16), tri_ref[...])
    rank_ref[...] = jnp.concatenate(
        [jnp.sum(jnp.where(hit, slot, 0.0), axis=0, keepdims=True) for _, hit in picks], axis=0).astype(I32)
    cout_ref[...] += jnp.sum(chosen, axis=1, keepdims=True)


def _route(h, wr_t, bias, count_in):
    n = h.shape[0]
    tm = min(TOKEN_TILE, n)
    assert n % tm == 0
    earlier = (lax.broadcasted_iota(I32, (tm, tm), 0) < lax.broadcasted_iota(I32, (tm, tm), 1)).astype(BF16)
    const = lambda shape: pl.BlockSpec(shape, lambda i: (0,) * len(shape))
    picks = pl.BlockSpec((TOP_K, tm), lambda i: (0, i))
    return pl.pallas_call(
        functools.partial(_route_kernel, tm=tm),
        grid=(n // tm,),
        in_specs=[pl.BlockSpec((tm,) + ROW_TILE, lambda i: (i, 0, 0)), const((N_EXPERTS, D_MODEL)),
                  const((N_EXPERTS, 1)), const((tm, tm)), const((N_EXPERTS, 1))],
        out_specs=[picks, picks, picks, const((N_EXPERTS, 1))],
        out_shape=[jax.ShapeDtypeStruct((TOP_K, n), I32), jax.ShapeDtypeStruct((TOP_K, n), F32),
                   jax.ShapeDtypeStruct((TOP_K, n), I32), jax.ShapeDtypeStruct((N_EXPERTS, 1), F32)],
        compiler_params=_params("arbitrary"),
        name="route",
    )(h, wr_t, bias, earlier, count_in)


def _slot_kernel(idx_ref, rank_ref, start_ref, dest_ref, *, tm):
    expert_id = lax.broadcasted_iota(I32, (N_EXPERTS, tm), 0)
    start = start_ref[...]
    rows = [jnp.sum(jnp.where(expert_id == idx_ref[j:j + 1, :], start, 0.0), axis=0, keepdims=True)
            for j in range(TOP_K)]
    dest_ref[...] = jnp.concatenate(rows, axis=0).astype(I32) + rank_ref[...]


def _slots(idx, rank, start):
    n = idx.shape[1]
    tm = min(TOKEN_TILE, n)
    assert n % tm == 0
    picks = pl.BlockSpec((TOP_K, tm), lambda i: (0, i))
    return pl.pallas_call(
        functools.partial(_slot_kernel, tm=tm),
        grid=(n // tm,),
        in_specs=[picks, picks, pl.BlockSpec((N_EXPERTS, 1), lambda i: (0, 0))],
        out_specs=picks,
        out_shape=jax.ShapeDtypeStruct((TOP_K, n), I32),
        compiler_params=_params("parallel"),
        name="slots",
    )(idx, rank, start)


def _start_row_copies(tm, copy_of):
    def group(g, carry):
        tok0 = pl.multiple_of(g * SUBLANES, SUBLANES)
        for r in range(SUBLANES):
            for j in range(TOP_K):
                copy_of(tok0 + r, j, tok0 * TOP_K + (r * TOP_K + j)).start(priority=(r + j) % 2)
        return carry

    lax.fori_loop(0, tm // SUBLANES, group, 0)


def _dispatch_kernel(dest_ref, h_ref, *rest, tm):
    xs_ref, sem = rest[-2], rest[-1]
    _start_row_copies(
        tm, lambda tok, j, pair: pltpu.make_async_copy(h_ref.at[tok], xs_ref.at[dest_ref[pair]], sem))
    for _ in range(TOP_K):
        pltpu.make_async_copy(h_ref, xs_ref.at[pl.ds(0, tm)], sem).wait()


def _dispatch(dest, h, xs, cap):
    n = h.shape[0]
    tm = min(GATHER_TILE, n)
    assert n % tm == 0
    in_specs = [pl.BlockSpec((tm * TOP_K,), lambda i: (i,), memory_space=pltpu.SMEM),
                pl.BlockSpec((tm,) + ROW_TILE, lambda i: (i, 0, 0))]
    args = [dest, h]
    aliases = {}
    if xs is not None:
        in_specs.append(pl.BlockSpec(memory_space=pl.ANY))
        args.append(xs)
        aliases = {2: 0}
    return pl.pallas_call(
        functools.partial(_dispatch_kernel, tm=tm),
        grid=(n // tm,),
        in_specs=in_specs,
        out_specs=pl.BlockSpec(memory_space=pl.ANY),
        out_shape=jax.ShapeDtypeStruct((cap,) + ROW_TILE, F32),
        scratch_shapes=[pltpu.SemaphoreType.DMA],
        input_output_aliases=aliases,
        compiler_params=pltpu.CompilerParams(dimension_semantics=("arbitrary",), vmem_limit_bytes=VMEM_LIMIT,
                                             has_side_effects=True),
        name="dispatch",
    )(*args)


def _expert_kernel(first_ref, nblk_ref, used_ref, wg_ref, wu_ref, wd_ref, xs_ref, ys_ref,
                   wg_b, wu_b, wd_b, x_buf, y_buf, x_sem, y_sem):
    e = pl.program_id(0)
    blk, nx, ny = EXPERT_BLOCK, EXPERT_X_SLOTS, EXPERT_Y_SLOTS
    first, n, used = first_ref[e], nblk_ref[e], used_ref[0]

    def x_copy(g):
        slot = g % nx
        return pltpu.make_async_copy(xs_ref.at[pl.ds(pl.multiple_of(g * blk, blk), blk)], x_buf.at[slot],
                                     x_sem.at[slot])

    def y_copy(g):
        slot = g % ny
        return pltpu.make_async_copy(y_buf.at[slot], ys_ref.at[pl.ds(pl.multiple_of(g * blk, blk), blk)],
                                     y_sem.at[slot])

    @pl.when(e == 0)
    def _():
        for g in range(nx - 1):
            @pl.when(g < used)
            def _():
                x_copy(g).start()

    @pl.when(n > 0)
    def _():
        wg_b[...] = wg_ref[0].astype(BF16)
        wu_b[...] = wu_ref[0].astype(BF16)
        wd_b[...] = wd_ref[0].astype(BF16)

        def block(b, carry):
            g = first + b

            @pl.when(g + (nx - 1) < used)
            def _():
                x_copy(g + (nx - 1)).start()

            x_copy(g).wait()
            x = _from_tiles(x_buf[g % nx]).astype(BF16)
            gate = _dot(x, wg_b[...])
            hid = gate * _sigmoid(gate) * _dot(x, wu_b[...])
            y = _dot(hid.astype(BF16), wd_b[...])

            @pl.when(g >= ny)
            def _():
                y_copy(g - ny).wait()

            y_buf[g % ny] = _to_tiles(y)
            y_copy(g).start()
            return carry

        lax.fori_loop(0, n, block, 0)

    @pl.when(e == pl.num_programs(0) - 1)
    def _():
        for back in range(ny, 0, -1):
            @pl.when(used >= back)
            def _():
                y_copy(used - back).wait()


def _experts(first, nblk, used, xs, w_gate, w_up, w_down):
    cap = xs.shape[0]
    blk = EXPERT_BLOCK
    gate_up = pl.BlockSpec((1, D_MODEL, EXPERT_DIM), lambda e, *_: (e, 0, 0))
    return pl.pallas_call(
        _expert_kernel,
        grid_spec=pltpu.PrefetchScalarGridSpec(
            num_scalar_prefetch=3,
            grid=(N_EXPERTS,),
            in_specs=[gate_up, gate_up, pl.BlockSpec((1, EXPERT_DIM, D_MODEL), lambda e, *_: (e, 0, 0)),
                      pl.BlockSpec(memory_space=pl.ANY)],
            out_specs=pl.BlockSpec(memory_space=pl.ANY),
            scratch_shapes=[pltpu.VMEM((D_MODEL, EXPERT_DIM), BF16), pltpu.VMEM((D_MODEL, EXPERT_DIM), BF16),
                            pltpu.VMEM((EXPERT_DIM, D_MODEL), BF16),
                            pltpu.VMEM((EXPERT_X_SLOTS, blk) + ROW_TILE, F32),
                            pltpu.VMEM((EXPERT_Y_SLOTS, blk) + ROW_TILE, F32),
                            pltpu.SemaphoreType.DMA((EXPERT_X_SLOTS,)), pltpu.SemaphoreType.DMA((EXPERT_Y_SLOTS,))],
        ),
        out_shape=jax.ShapeDtypeStruct((cap,) + ROW_TILE, F32),
        compiler_params=_params("arbitrary"),
        name="experts",
    )(first, nblk, used, w_gate, w_up, w_down, xs)


def _combine_kernel(dest0_ref, dest1_ref, h_ref, gate_ref, ys_ref, wsg_ref, wsu_ref, wsd_ref, g_ref, b_ref, o_ref,
                    rows_ref, sem, *, tm):
    i = pl.program_id(0)
    slot = i % 2

    def gather(dest_ref, into):
        _start_row_copies(tm, lambda tok, j, pair: pltpu.make_async_copy(
            ys_ref.at[dest_ref[pair]], rows_ref.at[into, j, tok], sem.at[into]))

    @pl.when(i == 0)
    def _():
        gather(dest0_ref, 0)

    @pl.when(i + 1 < pl.num_programs(0))
    def _():
        gather(dest1_ref, 1 - slot)

    h = _from_tiles(h_ref[...])
    hb = h.astype(BF16)
    gate = _dot(hb, wsg_ref[...])
    hid = gate * _sigmoid(gate) * _dot(hb, wsu_ref[...])
    ffn = _dot(hid.astype(BF16), wsd_ref[...])
    for j in range(TOP_K):
        pltpu.make_async_copy(ys_ref.at[pl.ds(0, tm)], rows_ref.at[slot, j], sem.at[slot]).wait()
    weights = gate_ref[...]
    for j in range(TOP_K):
        ffn = ffn + weights[:, j:j + 1] * _from_tiles(rows_ref[slot, j])
    o_ref[...] = _layer_norm(DEEPNORM_ALPHA * h + ffn, g_ref[...], b_ref[...])


def _combine(dest, h, gate_t, ys, wsg, wsu, wsd, ln_g, ln_b):
    n, d = h.shape[0], D_MODEL
    tm = min(GATHER_TILE, n)
    assert n % tm == 0
    const = lambda shape: pl.BlockSpec(shape, lambda i: (0,) * len(shape))
    steps = n // tm
    return pl.pallas_call(
        functools.partial(_combine_kernel, tm=tm),
        grid=(steps,),
        in_specs=[pl.BlockSpec((tm * TOP_K,), lambda i: (0,), memory_space=pltpu.SMEM),
                  pl.BlockSpec((tm * TOP_K,), lambda i: (jnp.minimum(i + 1, steps - 1),), memory_space=pltpu.SMEM),
                  pl.BlockSpec((tm,) + ROW_TILE, lambda i: (i, 0, 0)),
                  pl.BlockSpec((tm, TOP_K), lambda i: (i, 0)),
                  pl.BlockSpec(memory_space=pl.ANY),
                  const((d, EXPERT_DIM)), const((d, EXPERT_DIM)), const((EXPERT_DIM, d)),
                  const((1, d)), const((1, d))],
        out_specs=pl.BlockSpec((tm, d), lambda i: (i, 0)),
        out_shape=jax.ShapeDtypeStruct((n, d), F32),
        scratch_shapes=[pltpu.VMEM((2, TOP_K, tm) + ROW_TILE, F32), pltpu.SemaphoreType.DMA((2,))],
        compiler_params=_params("arbitrary"),
        name="combine",
    )(dest, dest, h, gate_t, ys, wsg, wsu, wsd, ln_g.reshape(1, d), ln_b.reshape(1, d))


def _key_major(cache):
    b, h, s, hd = cache.shape
    return cache.transpose(0, 2, 1, 3).reshape(b, s, h * hd).astype(BF16)


def kernel(x_prompt, x_sample, cache_k, cache_v, state_pool, ln0_g, ln0_b, w_in, w_sb_up, w_pool_grp, pool_scale,
           w_pool_up, w_o, ln1_g, ln1_b, w_router, router_bias, w_exp_gate, w_exp_up, w_exp_down, w_sh_gate,
           w_sh_up, w_sh_down, ln2_g, ln2_b):
    assert w_in.shape[0] == DEPTH
    b_p, t_p, d = x_prompt.shape
    b_s, t_s, _ = x_sample.shape
    past = cache_k.shape[3]
    n_p, n_s = b_p * t_p, b_s * t_s
    layer = 0
    w_in_b = w_in[layer].astype(BF16)
    mix_w = (w_sb_up[layer].astype(BF16), w_pool_grp[layer].astype(BF16), pool_scale[layer],
             w_pool_up[layer].astype(BF16), w_o[layer].astype(BF16), ln1_g[layer], ln1_b[layer])

    qb, kb, vb, k_p, v_p, u_p, sg_p = _proj(x_prompt, ln0_g, ln0_b, w_in_b)
    osb_p = _attn(qb, kb, vb, q_pos0=0)
    hist_p = jnp.zeros((b_p, HALO, POOL_WIDTH), F32)
    h_p = _merge(x_prompt, osb_p, u_p, hist_p, sg_p, ln0_g, ln0_b, *mix_w, hist_valid=0)

    qs, ks, vs, k_s, v_s, u_s, sg_s = _proj(x_sample, ln0_g, ln0_b, w_in_b)
    key_rows = -(-(past + t_s) // KEY_TILE) * KEY_TILE
    pad = ((0, 0), (0, key_rows - past - t_s), (0, 0))
    k_all = jnp.pad(jnp.concatenate([_key_major(cache_k[layer]), ks], axis=1), pad)
    v_all = jnp.pad(jnp.concatenate([_key_major(cache_v[layer]), vs], axis=1), pad)
    osb_s = _attn(qs, k_all, v_all, q_pos0=past)
    hist_s = jnp.concatenate([jnp.zeros((b_s, 1, POOL_WIDTH), F32), state_pool[layer]], axis=1)
    h_s = _merge(x_sample, osb_s, u_s, hist_s, sg_s, ln0_g, ln0_b, *mix_w, hist_valid=POOL_STATE)

    wr_t = w_router[layer].T.astype(BF16)
    bias = router_bias[layer].astype(F32).reshape(N_EXPERTS, 1)
    idx_p, gate_p, rank_p, count = _route(h_p, wr_t, bias, jnp.zeros((N_EXPERTS, 1), F32))
    idx_s, gate_s, rank_s, count = _route(h_s, wr_t, bias, count)

    blk = EXPERT_BLOCK
    n_blocks = ((n_p + n_s) * TOP_K + N_EXPERTS * (blk - 1) + blk - 1) // blk
    cap = n_blocks * blk
    assert cap < 2 ** 24
    nblk = (count[:, 0].astype(I32) + blk - 1) // blk
    first = jnp.cumsum(nblk) - nblk
    used = jnp.sum(nblk).reshape(1)
    start = (first * blk).astype(F32).reshape(N_EXPERTS, 1)
    dest_p = _slots(idx_p, rank_p, start).T.reshape(-1)
    dest_s = _slots(idx_s, rank_s, start).T.reshape(-1)

    xs = _dispatch(dest_p, h_p, None, cap)
    xs = _dispatch(dest_s, h_s, xs, cap)
    ys = _experts(first, nblk, used, xs, w_exp_gate[layer], w_exp_up[layer], w_exp_down[layer])

    shared_w = (w_sh_gate[layer].astype(BF16), w_sh_up[layer].astype(BF16), w_sh_down[layer].astype(BF16),
                ln2_g[layer], ln2_b[layer])
    y_p = _combine(dest_p, h_p, gate_p.T, ys, *shared_w)
    y_s = _combine(dest_s, h_s, gate_s.T, ys, *shared_w)

    new_pool_p = u_p[:, t_p - POOL_STATE:][None]
    new_pool_s = jnp.concatenate([state_pool[layer], u_s], axis=1)[:, -POOL_STATE:][None]
    return (y_p.reshape(b_p, t_p, d), y_s.reshape(b_s, t_s, d), k_p, v_p, new_pool_p, k_s, v_s, new_pool_s)
```

```python
import functools

import jax
import jax.numpy as jnp
from jax import lax
from jax.experimental import pallas as pl
from jax.experimental.pallas import tpu as pltpu

F32 = jnp.float32
BF16 = jnp.bfloat16
I32 = jnp.int32
U32 = jnp.uint32

D_MODEL = 1024
SB_HEADS = 8
SB_HEAD_DIM = 64
SB_WIDTH = SB_HEADS * SB_HEAD_DIM
SB_SCALE = SB_HEAD_DIM ** -0.5
POOL_WINDOWS = (2, 4, 8, 16)
POOL_WIDTH = D_MODEL // 2
POOL_GROUP_WIDTH = POOL_WIDTH // len(POOL_WINDOWS)
POOL_STATE = max(POOL_WINDOWS) - 1
HALO = POOL_STATE + 1
N_EXPERTS = 256
TOP_K = 8
N_EXPERT_GROUPS = 8
TOPK_GROUPS = 4
GROUP_SIZE = N_EXPERTS // N_EXPERT_GROUPS
EXPERT_DIM = D_MODEL // 4
ROUTED_SCALE = 2.5
LN_EPS = 1e-5
DEPTH = 1
DEEPNORM_ALPHA = (2 * DEPTH) ** 0.25

SUBLANES, LANES = 8, 128
ROW_TILE = (SUBLANES, LANES)
assert SUBLANES * LANES == D_MODEL
EXPERT_X_AHEAD = 3
EXPERT_X_SLOTS = EXPERT_X_AHEAD + 2
EXPERT_Y_SLOTS = 3
KEY_TILE = 128
MAX_QUERY_TILE = 128
EXPERT_BLOCK = 256
TOKEN_TILE = 512
GATHER_TILE = 256
EXP_UNDERFLOW = 104.0
VMEM_LIMIT = 56 * 1024 * 1024

_NT = (((1,), (1,)), ((), ()))


def _params(*semantics):
    return pltpu.CompilerParams(dimension_semantics=semantics, vmem_limit_bytes=VMEM_LIMIT)


def _layer_norm(x, g, b):
    mu = jnp.mean(x, axis=-1, keepdims=True)
    xc = x - mu
    var = jnp.mean(xc * xc, axis=-1, keepdims=True)
    return xc * lax.rsqrt(var + LN_EPS) * g + b


def _sigmoid(x):
    return 1.0 / (1.0 + jnp.exp(-x))


def _dot(a, b):
    return jnp.dot(a, b, preferred_element_type=F32)


def _to_tiles(x):
    return x.reshape((x.shape[0],) + ROW_TILE)


def _from_tiles(x):
    return x.reshape(x.shape[0], D_MODEL)


def _proj_kernel(x_ref, g_ref, b_ref, w_ref, qb_ref, kb_ref, vb_ref, k_ref, v_ref, u_ref, sg_ref):
    h = _layer_norm(x_ref[0], g_ref[...], b_ref[...]).astype(BF16)

    def proj(lo, hi):
        return _dot(h, w_ref[:, lo:hi])

    qb_ref[0] = (proj(0, SB_WIDTH) * SB_SCALE).astype(BF16)
    zk = proj(SB_WIDTH, 2 * SB_WIDTH)
    kb_ref[0] = zk.astype(BF16)
    zv = proj(2 * SB_WIDTH, 3 * SB_WIDTH)
    vb_ref[0] = zv.astype(BF16)
    for hd in range(SB_HEADS):
        sl = slice(hd * SB_HEAD_DIM, (hd + 1) * SB_HEAD_DIM)
        k_ref[0, 0, hd] = zk[:, sl]
        v_ref[0, 0, hd] = zv[:, sl]
    u0 = 3 * SB_WIDTH
    u_ref[0] = proj(u0, u0 + POOL_WIDTH)
    g0 = u0 + POOL_WIDTH
    for c in range(2 * D_MODEL // SB_WIDTH):
        sl = slice(c * SB_WIDTH, (c + 1) * SB_WIDTH)
        sg_ref[0, :, sl] = _sigmoid(proj(g0 + c * SB_WIDTH, g0 + (c + 1) * SB_WIDTH)).astype(BF16)


def _proj(x, ln_g, ln_b, w_in_b):
    b, t, d = x.shape
    tm = min(TOKEN_TILE, t)
    in_width = w_in_b.shape[1]
    row = lambda width: pl.BlockSpec((1, tm, width), lambda i, j: (i, j, 0))
    heads = pl.BlockSpec((1, 1, SB_HEADS, tm, SB_HEAD_DIM), lambda i, j: (0, i, 0, j, 0))
    const = lambda shape: pl.BlockSpec(shape, lambda i, j: (0,) * len(shape))
    head_shape = jax.ShapeDtypeStruct((1, b, SB_HEADS, t, SB_HEAD_DIM), F32)
    return pl.pallas_call(
        _proj_kernel,
        grid=(b, t // tm),
        in_specs=[row(d), const((1, d)), const((1, d)), const((d, in_width))],
        out_specs=[row(SB_WIDTH), row(SB_WIDTH), row(SB_WIDTH), heads, heads, row(POOL_WIDTH), row(2 * d)],
        out_shape=[
            jax.ShapeDtypeStruct((b, t, SB_WIDTH), BF16),
            jax.ShapeDtypeStruct((b, t, SB_WIDTH), BF16),
            jax.ShapeDtypeStruct((b, t, SB_WIDTH), BF16),
            head_shape,
            head_shape,
            jax.ShapeDtypeStruct((b, t, POOL_WIDTH), F32),
            jax.ShapeDtypeStruct((b, t, 2 * d), BF16),
        ],
        compiler_params=_params("parallel", "parallel"),
        name="proj",
    )(x, ln_g.reshape(1, d), ln_b.reshape(1, d), w_in_b)


def _attn_kernel(q_ref, k_ref, v_ref, tri_ref, o_ref, acc_ref, carry_ref, qm_ref, *, tq, q_pos0):
    tk = KEY_TILE
    pairs = SB_HEADS // 2
    q_first = q_pos0 + pl.program_id(1) * tq
    low_half = lax.broadcasted_iota(I32, (1, 2 * SB_HEAD_DIM), 1) < SB_HEAD_DIM

    def split_heads(x2):
        zero = jnp.zeros_like(x2)
        return jnp.concatenate([jnp.where(low_half, x2, zero), jnp.where(low_half, zero, x2)], axis=0)

    acc_ref[...] = jnp.zeros_like(acc_ref)
    carry_ref[...] = jnp.zeros_like(carry_ref)
    for pair in range(pairs):
        qm_ref[pair] = split_heads(q_ref[0, :, pair * 2 * SB_HEAD_DIM:(pair + 1) * 2 * SB_HEAD_DIM])

    def step(j, masked):
        k0 = pl.multiple_of(j * tk, tk)
        k_tile = k_ref[0, pl.ds(k0, tk), :]
        v_tile = v_ref[0, pl.ds(k0, tk), :]
        s = jnp.concatenate(
            [lax.dot_general(qm_ref[pair], k_tile[:, pair * 2 * SB_HEAD_DIM:(pair + 1) * 2 * SB_HEAD_DIM], _NT,
                             preferred_element_type=F32) for pair in range(pairs)],
            axis=0).reshape(SB_HEADS, tq, tk)
        soft = jnp.log(1.0 + jnp.exp(-jnp.abs(s)))
        log_rest = -(jnp.maximum(s, 0.0) + soft)
        log_beta = log_rest + s
        if masked:
            mask = ((k0 + lax.broadcasted_iota(I32, (tq, tk), 1))
                    < (q_first + lax.broadcasted_iota(I32, (tq, tk), 0)))[None]
            log_rest = jnp.where(mask, log_rest, 0.0)
        hi = log_rest.astype(BF16)
        lo = (log_rest - hi.astype(F32)).astype(BF16)
        sums = _dot(jnp.concatenate([hi, lo], axis=-1).reshape(SB_HEADS * tq, 2 * tk), tri_ref[...])
        sums = sums.reshape(SB_HEADS, tq, 2 * tk)
        carry = carry_ref[...]
        a = jnp.exp(log_beta + sums[:, :, :tk] + carry)
        if masked:
            a = jnp.where(mask, a, 0.0)
        a = a.astype(BF16)
        carry = carry + sums[:, :, tk:]
        carry_ref[...] = carry
        for pair in range(pairs):
            lanes = slice(pair * 2 * SB_HEAD_DIM, (pair + 1) * 2 * SB_HEAD_DIM)
            a2 = jnp.concatenate([a[2 * pair], a[2 * pair + 1]], axis=1)
            acc_ref[:, lanes] += _dot(a2, split_heads(v_tile[:, lanes]))
        return (jnp.max(carry) > -EXP_UNDERFLOW).astype(I32)

    j_first = (q_first + tq - 1) // tk
    go = step(j_first, True)
    lax.while_loop(lambda st: jnp.logical_and(st[0] >= 0, st[1] > 0),
                   lambda st: (st[0] - 1, step(st[0], False)), (j_first - 1, go))
    o_ref[0] = acc_ref[...].astype(BF16)


def _attn(qb, kb, vb, *, q_pos0):
    b, t, w = qb.shape
    s_k = kb.shape[1]
    tq = min(MAX_QUERY_TILE, t)
    tk = KEY_TILE
    assert t % tq == 0 and s_k % tk == 0 and s_k >= q_pos0 + t
    assert tk % tq == 0 and q_pos0 % tq == 0
    r = lax.broadcasted_iota(I32, (2 * tk, 2 * tk), 0) % tk
    c = lax.broadcasted_iota(I32, (2 * tk, 2 * tk), 1)
    tri = jnp.logical_or(r > c, c >= tk).astype(BF16)
    return pl.pallas_call(
        functools.partial(_attn_kernel, tq=tq, q_pos0=q_pos0),
        grid=(b, t // tq),
        in_specs=[
            pl.BlockSpec((1, tq, w), lambda i, j: (i, j, 0)),
            pl.BlockSpec((1, s_k, w), lambda i, j: (i, 0, 0)),
            pl.BlockSpec((1, s_k, w), lambda i, j: (i, 0, 0)),
            pl.BlockSpec((2 * tk, 2 * tk), lambda i, j: (0, 0)),
        ],
        out_specs=pl.BlockSpec((1, tq, w), lambda i, j: (i, j, 0)),
        out_shape=jax.ShapeDtypeStruct((b, t, w), BF16),
        scratch_shapes=[pltpu.VMEM((tq, w), F32), pltpu.VMEM((SB_HEADS, tq, tk), F32),
                        pltpu.VMEM((SB_HEADS // 2, 2 * tq, 2 * SB_HEAD_DIM), BF16)],
        compiler_params=_params("parallel", "parallel"),
        name="attn",
    )(qb, kb, vb, tri)


def _merge_kernel(x_ref, osb_ref, u_ref, uprev_ref, hist_ref, sg_ref, g0_ref, b0_ref, wsb_ref, wgrp_ref,
                  pscale_ref, wpool_ref, wo_ref, g1_ref, b1_ref, h_ref, ext_ref, *, tm, hist_valid):
    t = pl.program_id(1)
    h0 = _layer_norm(x_ref[0], g0_ref[...], b0_ref[...])
    u = u_ref[0]
    ext_ref[:HALO, :] = jnp.where(t == 0, hist_ref[0], uprev_ref[0])
    ext_ref[HALO:, :] = u
    rows_seen = t * tm + lax.broadcasted_iota(I32, (tm, 1), 0) + (1 + hist_valid)
    pool_out = []
    for g, window in enumerate(POOL_WINDOWS):
        lanes = slice(g * POOL_GROUP_WIDTH, (g + 1) * POOL_GROUP_WIDTH)
        total = u[:, lanes]
        for back in range(1, window):
            total = total + ext_ref[HALO - back:HALO - back + tm, lanes]
        count = jnp.minimum(rows_seen, window).astype(F32)
        pooled = total / count - u[:, lanes]
        pool_out.append(_dot(pooled.astype(BF16), wgrp_ref[g]))
    o_pool = jnp.concatenate(pool_out, axis=1) * pscale_ref[...]
    a = _dot(osb_ref[0], wsb_ref[...])
    p = _dot(o_pool.astype(BF16), wpool_ref[...])
    sg = sg_ref[0].astype(F32)
    m = sg[:, :D_MODEL] * a + sg[:, D_MODEL:] * p
    mix = _dot(m.astype(BF16), wo_ref[...])
    h1 = _layer_norm(DEEPNORM_ALPHA * h0 + mix, g1_ref[...], b1_ref[...])
    h_ref[...] = _to_tiles(h1)


def _merge(x, osb, u, hist, sg, ln0_g, ln0_b, wsb, wgrp, pscale, wpool, wo, ln1_g, ln1_b, *, hist_valid):
    b, t, d = x.shape
    tm = min(TOKEN_TILE, t)
    nt = t // tm
    row = lambda width: pl.BlockSpec((1, tm, width), lambda i, j: (i, j, 0))
    const = lambda shape: pl.BlockSpec(shape, lambda i, j: (0,) * len(shape))
    prev = pl.BlockSpec((1, HALO, POOL_WIDTH), lambda i, j: (i, jnp.maximum(j * (tm // HALO) - 1, 0), 0))
    return pl.pallas_call(
        functools.partial(_merge_kernel, tm=tm, hist_valid=hist_valid),
        grid=(b, nt),
        in_specs=[
            row(d), row(SB_WIDTH), row(POOL_WIDTH), prev,
            pl.BlockSpec((1, HALO, POOL_WIDTH), lambda i, j: (i, 0, 0)),
            row(2 * d), const((1, d)), const((1, d)), const((SB_WIDTH, d)),
            const((len(POOL_WINDOWS), POOL_GROUP_WIDTH, POOL_GROUP_WIDTH)), const((1, POOL_WIDTH)),
            const((POOL_WIDTH, d)), const((d, d)), const((1, d)), const((1, d)),
        ],
        out_specs=pl.BlockSpec((tm,) + ROW_TILE, lambda i, j: (i * nt + j, 0, 0)),
        out_shape=jax.ShapeDtypeStruct((b * t,) + ROW_TILE, F32),
        scratch_shapes=[pltpu.VMEM((HALO + tm, POOL_WIDTH), F32)],
        compiler_params=_params("parallel", "parallel"),
        name="merge",
    )(x, osb, u, u, hist, sg, ln0_g.reshape(1, d), ln0_b.reshape(1, d), wsb, wgrp, pscale.reshape(1, -1),
      wpool, wo, ln1_g.reshape(1, d), ln1_b.reshape(1, d))


def _first_max(vals, index, n):
    top = jnp.max(vals, axis=0, keepdims=True)
    first = jnp.min(jnp.where(vals == top, index, n), axis=0, keepdims=True)
    return top, first


def _route_kernel(h_ref, wr_ref, bias_ref, tri_ref, cin_ref, idx_ref, gate_ref, rank_ref, cout_ref, *, tm):
    @pl.when(pl.program_id(0) == 0)
    def _():
        cout_ref[...] = cin_ref[...]

    h = _from_tiles(h_ref[...]).astype(BF16)
    scores = _sigmoid(lax.dot_general(wr_ref[...], h, _NT, preferred_element_type=F32))
    biased = scores + bias_ref[...]
    neg = jnp.float32(-jnp.inf)

    in_group = lax.broadcasted_iota(I32, (GROUP_SIZE, tm), 0)
    group_scores = []
    for g in range(N_EXPERT_GROUPS):
        blk = biased[g * GROUP_SIZE:(g + 1) * GROUP_SIZE, :]
        top, first = _first_max(blk, in_group, GROUP_SIZE)
        second = jnp.max(jnp.where(in_group == first, neg, blk), axis=0, keepdims=True)
        group_scores.append(top + second)
    remaining = jnp.concatenate(group_scores, axis=0)
    group_id = lax.broadcasted_iota(I32, (N_EXPERT_GROUPS, tm), 0)
    kept = jnp.zeros((N_EXPERT_GROUPS, tm), F32)
    for _ in range(TOPK_GROUPS):
        _, first = _first_max(remaining, group_id, N_EXPERT_GROUPS)
        hit = group_id == first
        kept = jnp.where(hit, 1.0, kept)
        remaining = jnp.where(hit, neg, remaining)
    remaining = jnp.concatenate(
        [jnp.where(kept[g:g + 1, :] > 0.5, biased[g * GROUP_SIZE:(g + 1) * GROUP_SIZE, :], neg)
         for g in range(N_EXPERT_GROUPS)], axis=0)

    expert_id = lax.broadcasted_iota(I32, (N_EXPERTS, tm), 0)
    chosen, picks, picked_scores = jnp.zeros((N_EXPERTS, tm), F32), [], []
    for _ in range(TOP_K):
        _, first = _first_max(remaining, expert_id, N_EXPERTS)
        hit = expert_id == first
        picks.append((first, hit))
        picked_scores.append(jnp.sum(jnp.where(hit, scores, 0.0), axis=0, keepdims=True))
        chosen = jnp.where(hit, 1.0, chosen)
        remaining = jnp.where(hit, neg, remaining)
    denom = picked_scores[0]
    for sc in picked_scores[1:]:
        denom = denom + sc
    gate_ref[...] = jnp.concatenate([sc / denom * ROUTED_SCALE for sc in picked_scores], axis=0)
    idx_ref[...] = jnp.concatenate([first for first, _ in picks], axis=0)

    slot = cout_ref[...] + _dot(chosen.astype(BF16), tri_ref[...])
    rank_ref[...] = jnp.concatenate(
        [jnp.sum(jnp.where(hit, slot, 0.0), axis=0, keepdims=True) for _, hit in picks], axis=0).astype(I32)
    cout_ref[...] += jnp.sum(chosen, axis=1, keepdims=True)


def _route(h, wr_t, bias, count_in):
    n = h.shape[0]
    tm = min(TOKEN_TILE, n)
    assert n % tm == 0
    earlier = (lax.broadcasted_iota(I32, (tm, tm), 0) < lax.broadcasted_iota(I32, (tm, tm), 1)).astype(BF16)
    const = lambda shape: pl.BlockSpec(shape, lambda i: (0,) * len(shape))
    picks = pl.BlockSpec((TOP_K, tm), lambda i: (0, i))
    return pl.pallas_call(
        functools.partial(_route_kernel, tm=tm),
        grid=(n // tm,),
        in_specs=[pl.BlockSpec((tm,) + ROW_TILE, lambda i: (i, 0, 0)), const((N_EXPERTS, D_MODEL)),
                  const((N_EXPERTS, 1)), const((tm, tm)), const((N_EXPERTS, 1))],
        out_specs=[picks, picks, picks, const((N_EXPERTS, 1))],
        out_shape=[jax.ShapeDtypeStruct((TOP_K, n), I32), jax.ShapeDtypeStruct((TOP_K, n), F32),
                   jax.ShapeDtypeStruct((TOP_K, n), I32), jax.ShapeDtypeStruct((N_EXPERTS, 1), F32)],
        compiler_params=_params("arbitrary"),
        name="route",
    )(h, wr_t, bias, earlier, count_in)


def _slot_kernel(idx_ref, rank_ref, start_ref, dest_ref, *, tm):
    expert_id = lax.broadcasted_iota(I32, (N_EXPERTS, tm), 0)
    start = start_ref[...]
    rows = [jnp.sum(jnp.where(expert_id == idx_ref[j:j + 1, :], start, 0.0), axis=0, keepdims=True)
            for j in range(TOP_K)]
    dest_ref[...] = jnp.concatenate(rows, axis=0).astype(I32) + rank_ref[...]


def _slots(idx, rank, start):
    n = idx.shape[1]
    tm = min(TOKEN_TILE, n)
    assert n % tm == 0
    picks = pl.BlockSpec((TOP_K, tm), lambda i: (0, i))
    return pl.pallas_call(
        functools.partial(_slot_kernel, tm=tm),
        grid=(n // tm,),
        in_specs=[picks, picks, pl.BlockSpec((N_EXPERTS, 1), lambda i: (0, 0))],
        out_specs=picks,
        out_shape=jax.ShapeDtypeStruct((TOP_K, n), I32),
        compiler_params=_params("parallel"),
        name="slots",
    )(idx, rank, start)


def _start_row_copies(tm, copy_of):
    def group(g, carry):
        tok0 = pl.multiple_of(g * SUBLANES, SUBLANES)
        for r in range(SUBLANES):
            for j in range(TOP_K):
                copy_of(tok0 + r, j, tok0 * TOP_K + (r * TOP_K + j)).start(priority=(r + j) % 2)
        return carry

    lax.fori_loop(0, tm // SUBLANES, group, 0)


def _dispatch_kernel(dest_ref, h_ref, *rest, tm):
    xs_ref, sem = rest[-2], rest[-1]
    _start_row_copies(
        tm, lambda tok, j, pair: pltpu.make_async_copy(h_ref.at[tok], xs_ref.at[dest_ref[pair]], sem))
    for _ in range(TOP_K):
        pltpu.make_async_copy(h_ref, xs_ref.at[pl.ds(0, tm)], sem).wait()


def _dispatch(dest, h, xs, cap):
    n = h.shape[0]
    tm = min(GATHER_TILE, n)
    assert n % tm == 0
    in_specs = [pl.BlockSpec((tm * TOP_K,), lambda i: (i,), memory_space=pltpu.SMEM),
                pl.BlockSpec((tm,) + ROW_TILE, lambda i: (i, 0, 0))]
    args = [dest, h]
    aliases = {}
    if xs is not None:
        in_specs.append(pl.BlockSpec(memory_space=pl.ANY))
        args.append(xs)
        aliases = {2: 0}
    return pl.pallas_call(
        functools.partial(_dispatch_kernel, tm=tm),
        grid=(n // tm,),
        in_specs=in_specs,
        out_specs=pl.BlockSpec(memory_space=pl.ANY),
        out_shape=jax.ShapeDtypeStruct((cap,) + ROW_TILE, F32),
        scratch_shapes=[pltpu.SemaphoreType.DMA],
        input_output_aliases=aliases,
        compiler_params=pltpu.CompilerParams(dimension_semantics=("arbitrary",), vmem_limit_bytes=VMEM_LIMIT,
                                             has_side_effects=True),
        name="dispatch",
    )(*args)


def _expert_kernel(first_ref, nblk_ref, used_ref, wg_ref, wu_ref, wd_ref, xs_ref, ys_ref,
                   wg_b, wu_b, wd_b, x_buf, y_buf, x_sem, y_sem):
    e = pl.program_id(0)
    blk, nx, ny, ahead = EXPERT_BLOCK, EXPERT_X_SLOTS, EXPERT_Y_SLOTS, EXPERT_X_AHEAD
    first, n, used = first_ref[e], nblk_ref[e], used_ref[0]

    def x_copy(g):
        slot = g % nx
        return pltpu.make_async_copy(xs_ref.at[pl.ds(pl.multiple_of(g * blk, blk), blk)], x_buf.at[slot],
                                     x_sem.at[slot])

    def y_copy(g):
        slot = g % ny
        return pltpu.make_async_copy(y_buf.at[slot], ys_ref.at[pl.ds(pl.multiple_of(g * blk, blk), blk)],
                                     y_sem.at[slot])

    @pl.when(e == 0)
    def _():
        for g in range(ahead):
            @pl.when(g < used)
            def _():
                x_copy(g).start()

    def compute(blocks):
        for g in blocks:
            @pl.when(g + ahead < used)
            def _():
                x_copy(g + ahead).start()
        for g in blocks:
            x_copy(g).wait()
        x = jnp.concatenate([_from_tiles(x_buf[g % nx]) for g in blocks], axis=0).astype(BF16)
        gate = _dot(x, wg_b[...])
        hid = gate * _sigmoid(gate) * _dot(x, wu_b[...])
        y = _dot(hid.astype(BF16), wd_b[...])
        for k, g in enumerate(blocks):
            @pl.when(g >= ny)
            def _():
                y_copy(g - ny).wait()

            y_buf[g % ny] = _to_tiles(y[k * blk:(k + 1) * blk])
            y_copy(g).start()

    @pl.when(n > 0)
    def _():
        wg_b[...] = wg_ref[0].astype(BF16)
        wu_b[...] = wu_ref[0].astype(BF16)
        wd_b[...] = wd_ref[0].astype(BF16)

        def pair(p, carry):
            compute([first + 2 * p, first + 2 * p + 1])
            return carry

        lax.fori_loop(0, n // 2, pair, 0)

        @pl.when(n % 2 == 1)
        def _():
            compute([first + n - 1])

    @pl.when(e == pl.num_programs(0) - 1)
    def _():
        for back in range(ny, 0, -1):
            @pl.when(used >= back)
            def _():
                y_copy(used - back).wait()


def _experts(first, nblk, used, xs, w_gate, w_up, w_down):
    cap = xs.shape[0]
    blk = EXPERT_BLOCK
    gate_up = pl.BlockSpec((1, D_MODEL, EXPERT_DIM), lambda e, *_: (e, 0, 0))
    return pl.pallas_call(
        _expert_kernel,
        grid_spec=pltpu.PrefetchScalarGridSpec(
            num_scalar_prefetch=3,
            grid=(N_EXPERTS,),
            in_specs=[gate_up, gate_up, pl.BlockSpec((1, EXPERT_DIM, D_MODEL), lambda e, *_: (e, 0, 0)),
                      pl.BlockSpec(memory_space=pl.ANY)],
            out_specs=pl.BlockSpec(memory_space=pl.ANY),
            scratch_shapes=[pltpu.VMEM((D_MODEL, EXPERT_DIM), BF16), pltpu.VMEM((D_MODEL, EXPERT_DIM), BF16),
                            pltpu.VMEM((EXPERT_DIM, D_MODEL), BF16),
                            pltpu.VMEM((EXPERT_X_SLOTS, blk) + ROW_TILE, F32),
                            pltpu.VMEM((EXPERT_Y_SLOTS, blk) + ROW_TILE, F32),
                            pltpu.SemaphoreType.DMA((EXPERT_X_SLOTS,)), pltpu.SemaphoreType.DMA((EXPERT_Y_SLOTS,))],
        ),
        out_shape=jax.ShapeDtypeStruct((cap,) + ROW_TILE, F32),
        compiler_params=_params("arbitrary"),
        name="experts",
    )(first, nblk, used, w_gate, w_up, w_down, xs)


def _combine_kernel(dest0_ref, dest1_ref, h_ref, gate_ref, ys_ref, wsg_ref, wsu_ref, wsd_ref, g_ref, b_ref, o_ref,
                    rows_ref, sem, *, tm):
    i = pl.program_id(0)
    slot = i % 2

    def gather(dest_ref, into):
        _start_row_copies(tm, lambda tok, j, pair: pltpu.make_async_copy(
            ys_ref.at[dest_ref[pair]], rows_ref.at[into, j, tok], sem.at[into]))

    @pl.when(i == 0)
    def _():
        gather(dest0_ref, 0)

    @pl.when(i + 1 < pl.num_programs(0))
    def _():
        gather(dest1_ref, 1 - slot)

    h = _from_tiles(h_ref[...])
    hb = h.astype(BF16)
    gate = _dot(hb, wsg_ref[...])
    hid = gate * _sigmoid(gate) * _dot(hb, wsu_ref[...])
    ffn = _dot(hid.astype(BF16), wsd_ref[...])
    for j in range(TOP_K):
        pltpu.make_async_copy(ys_ref.at[pl.ds(0, tm)], rows_ref.at[slot, j], sem.at[slot]).wait()
    weights = gate_ref[...]
    for j in range(TOP_K):
        ffn = ffn + weights[:, j:j + 1] * _from_tiles(rows_ref[slot, j])
    o_ref[...] = _layer_norm(DEEPNORM_ALPHA * h + ffn, g_ref[...], b_ref[...])


def _combine(dest, h, gate_t, ys, wsg, wsu, wsd, ln_g, ln_b):
    n, d = h.shape[0], D_MODEL
    tm = min(GATHER_TILE, n)
    assert n % tm == 0
    const = lambda shape: pl.BlockSpec(shape, lambda i: (0,) * len(shape))
    steps = n // tm
    return pl.pallas_call(
        functools.partial(_combine_kernel, tm=tm),
        grid=(steps,),
        in_specs=[pl.BlockSpec((tm * TOP_K,), lambda i: (0,), memory_space=pltpu.SMEM),
                  pl.BlockSpec((tm * TOP_K,), lambda i: (jnp.minimum(i + 1, steps - 1),), memory_space=pltpu.SMEM),
                  pl.BlockSpec((tm,) + ROW_TILE, lambda i: (i, 0, 0)),
                  pl.BlockSpec((tm, TOP_K), lambda i: (i, 0)),
                  pl.BlockSpec(memory_space=pl.ANY),
                  const((d, EXPERT_DIM)), const((d, EXPERT_DIM)), const((EXPERT_DIM, d)),
                  const((1, d)), const((1, d))],
        out_specs=pl.BlockSpec((tm, d), lambda i: (i, 0)),
        out_shape=jax.ShapeDtypeStruct((n, d), F32),
        scratch_shapes=[pltpu.VMEM((2, TOP_K, tm) + ROW_TILE, F32), pltpu.SemaphoreType.DMA((2,))],
        compiler_params=_params("arbitrary"),
        name="combine",
    )(dest, dest, h, gate_t, ys, wsg, wsu, wsd, ln_g.reshape(1, d), ln_b.reshape(1, d))


def _key_major(cache):
    b, h, s, hd = cache.shape
    return cache.transpose(0, 2, 1, 3).reshape(b, s, h * hd).astype(BF16)


def kernel(x_prompt, x_sample, cache_k, cache_v, state_pool, ln0_g, ln0_b, w_in, w_sb_up, w_pool_grp, pool_scale,
           w_pool_up, w_o, ln1_g, ln1_b, w_router, router_bias, w_exp_gate, w_exp_up, w_exp_down, w_sh_gate,
           w_sh_up, w_sh_down, ln2_g, ln2_b):
    assert w_in.shape[0] == DEPTH
    b_p, t_p, d = x_prompt.shape
    b_s, t_s, _ = x_sample.shape
    past = cache_k.shape[3]
    n_p, n_s = b_p * t_p, b_s * t_s
    layer = 0
    w_in_b = w_in[layer].astype(BF16)
    mix_w = (w_sb_up[layer].astype(BF16), w_pool_grp[layer].astype(BF16), pool_scale[layer],
             w_pool_up[layer].astype(BF16), w_o[layer].astype(BF16), ln1_g[layer], ln1_b[layer])

    qb, kb, vb, k_p, v_p, u_p, sg_p = _proj(x_prompt, ln0_g, ln0_b, w_in_b)
    osb_p = _attn(qb, kb, vb, q_pos0=0)
    hist_p = jnp.zeros((b_p, HALO, POOL_WIDTH), F32)
    h_p = _merge(x_prompt, osb_p, u_p, hist_p, sg_p, ln0_g, ln0_b, *mix_w, hist_valid=0)

    qs, ks, vs, k_s, v_s, u_s, sg_s = _proj(x_sample, ln0_g, ln0_b, w_in_b)
    key_rows = -(-(past + t_s) // KEY_TILE) * KEY_TILE
    pad = ((0, 0), (0, key_rows - past - t_s), (0, 0))
    k_all = jnp.pad(jnp.concatenate([_key_major(cache_k[layer]), ks], axis=1), pad)
    v_all = jnp.pad(jnp.concatenate([_key_major(cache_v[layer]), vs], axis=1), pad)
    osb_s = _attn(qs, k_all, v_all, q_pos0=past)
    hist_s = jnp.concatenate([jnp.zeros((b_s, 1, POOL_WIDTH), F32), state_pool[layer]], axis=1)
    h_s = _merge(x_sample, osb_s, u_s, hist_s, sg_s, ln0_g, ln0_b, *mix_w, hist_valid=POOL_STATE)

    wr_t = w_router[layer].T.astype(BF16)
    bias = router_bias[layer].astype(F32).reshape(N_EXPERTS, 1)
    idx_p, gate_p, rank_p, count = _route(h_p, wr_t, bias, jnp.zeros((N_EXPERTS, 1), F32))
    idx_s, gate_s, rank_s, count = _route(h_s, wr_t, bias, count)

    blk = EXPERT_BLOCK
    n_blocks = ((n_p + n_s) * TOP_K + N_EXPERTS * (blk - 1) + blk - 1) // blk
    cap = n_blocks * blk
    assert cap < 2 ** 24
    nblk = (count[:, 0].astype(I32) + blk - 1) // blk
    first = jnp.cumsum(nblk) - nblk
    used = jnp.sum(nblk).reshape(1)
    start = (first * blk).astype(F32).reshape(N_EXPERTS, 1)
    dest_p = _slots(idx_p, rank_p, start).T.reshape(-1)
    dest_s = _slots(idx_s, rank_s, start).T.reshape(-1)

    xs = _dispatch(dest_p, h_p, None, cap)
    xs = _dispatch(dest_s, h_s, xs, cap)
    ys = _experts(first, nblk, used, xs, w_exp_gate[layer], w_exp_up[layer], w_exp_down[layer])

    shared_w = (w_sh_gate[layer].astype(BF16), w_sh_up[layer].astype(BF16), w_sh_down[layer].astype(BF16),
                ln2_g[layer], ln2_b[layer])
    y_p = _combine(dest_p, h_p, gate_p.T, ys, *shared_w)
    y_s = _combine(dest_s, h_s, gate_s.T, ys, *shared_w)

    new_pool_p = u_p[:, t_p - POOL_STATE:][None]
    new_pool_s = jnp.concatenate([state_pool[layer], u_s], axis=1)[:, -POOL_STATE:][None]
    return (y_p.reshape(b_p, t_p, d), y_s.reshape(b_s, t_s, d), k_p, v_p, new_pool_p, k_s, v_s, new_pool_s)
```

```python
import functools

import jax
import jax.numpy as jnp
from jax import lax
from jax.experimental import pallas as pl
from jax.experimental.pallas import tpu as pltpu

F32 = jnp.float32
BF16 = jnp.bfloat16
I32 = jnp.int32
U32 = jnp.uint32

D_MODEL = 1024
SB_HEADS = 8
SB_HEAD_DIM = 64
SB_WIDTH = SB_HEADS * SB_HEAD_DIM
SB_SCALE = SB_HEAD_DIM ** -0.5
POOL_WINDOWS = (2, 4, 8, 16)
POOL_WIDTH = D_MODEL // 2
POOL_GROUP_WIDTH = POOL_WIDTH // len(POOL_WINDOWS)
POOL_STATE = max(POOL_WINDOWS) - 1
HALO = POOL_STATE + 1
N_EXPERTS = 256
TOP_K = 8
N_EXPERT_GROUPS = 8
TOPK_GROUPS = 4
GROUP_SIZE = N_EXPERTS // N_EXPERT_GROUPS
EXPERT_DIM = D_MODEL // 4
ROUTED_SCALE = 2.5
LN_EPS = 1e-5
DEPTH = 1
DEEPNORM_ALPHA = (2 * DEPTH) ** 0.25

SUBLANES, LANES = 8, 128
ROW_PAIR_TILE = (2 * SUBLANES, LANES)
assert SUBLANES * LANES == D_MODEL
EXPERT_X_AHEAD = 3
EXPERT_X_SLOTS = EXPERT_X_AHEAD + 2
EXPERT_Y_SLOTS = 3
KEY_TILE = 128
MAX_QUERY_TILE = 128
EXPERT_BLOCK = 256
TOKEN_TILE = 512
GATHER_TILE = 256
EXP_UNDERFLOW = 104.0
VMEM_LIMIT = 56 * 1024 * 1024

_NT = (((1,), (1,)), ((), ()))


def _params(*semantics):
    return pltpu.CompilerParams(dimension_semantics=semantics, vmem_limit_bytes=VMEM_LIMIT)


def _layer_norm(x, g, b):
    mu = jnp.mean(x, axis=-1, keepdims=True)
    xc = x - mu
    var = jnp.mean(xc * xc, axis=-1, keepdims=True)
    return xc * lax.rsqrt(var + LN_EPS) * g + b


def _sigmoid(x):
    return 1.0 / (1.0 + jnp.exp(-x))


def _dot(a, b):
    return jnp.dot(a, b, preferred_element_type=F32)


def _tiles_shape(rows):
    return (rows // 2,) + ROW_PAIR_TILE


def _to_tiles(x):
    return x.astype(BF16).reshape(_tiles_shape(x.shape[0]))


def _from_tiles(x):
    return x.reshape(2 * x.shape[0], D_MODEL)


def _tile_row(ref, tile, sub):
    if not isinstance(sub, int):
        sub = pl.multiple_of(sub, SUBLANES)
    return ref.at[tile, pl.ds(sub, SUBLANES), :]


def _proj_kernel(x_ref, g_ref, b_ref, w_ref, qb_ref, kb_ref, vb_ref, k_ref, v_ref, u_ref, sg_ref):
    h = _layer_norm(x_ref[0], g_ref[...], b_ref[...]).astype(BF16)

    def proj(lo, hi):
        return _dot(h, w_ref[:, lo:hi])

    qb_ref[0] = (proj(0, SB_WIDTH) * SB_SCALE).astype(BF16)
    zk = proj(SB_WIDTH, 2 * SB_WIDTH)
    kb_ref[0] = zk.astype(BF16)
    zv = proj(2 * SB_WIDTH, 3 * SB_WIDTH)
    vb_ref[0] = zv.astype(BF16)
    for hd in range(SB_HEADS):
        sl = slice(hd * SB_HEAD_DIM, (hd + 1) * SB_HEAD_DIM)
        k_ref[0, 0, hd] = zk[:, sl]
        v_ref[0, 0, hd] = zv[:, sl]
    u0 = 3 * SB_WIDTH
    u_ref[0] = proj(u0, u0 + POOL_WIDTH)
    g0 = u0 + POOL_WIDTH
    for c in range(2 * D_MODEL // SB_WIDTH):
        sl = slice(c * SB_WIDTH, (c + 1) * SB_WIDTH)
        sg_ref[0, :, sl] = _sigmoid(proj(g0 + c * SB_WIDTH, g0 + (c + 1) * SB_WIDTH)).astype(BF16)


def _proj(x, ln_g, ln_b, w_in_b):
    b, t, d = x.shape
    tm = min(TOKEN_TILE, t)
    in_width = w_in_b.shape[1]
    row = lambda width: pl.BlockSpec((1, tm, width), lambda i, j: (i, j, 0))
    heads = pl.BlockSpec((1, 1, SB_HEADS, tm, SB_HEAD_DIM), lambda i, j: (0, i, 0, j, 0))
    const = lambda shape: pl.BlockSpec(shape, lambda i, j: (0,) * len(shape))
    head_shape = jax.ShapeDtypeStruct((1, b, SB_HEADS, t, SB_HEAD_DIM), F32)
    return pl.pallas_call(
        _proj_kernel,
        grid=(b, t // tm),
        in_specs=[row(d), const((1, d)), const((1, d)), const((d, in_width))],
        out_specs=[row(SB_WIDTH), row(SB_WIDTH), row(SB_WIDTH), heads, heads, row(POOL_WIDTH), row(2 * d)],
        out_shape=[
            jax.ShapeDtypeStruct((b, t, SB_WIDTH), BF16),
            jax.ShapeDtypeStruct((b, t, SB_WIDTH), BF16),
            jax.ShapeDtypeStruct((b, t, SB_WIDTH), BF16),
            head_shape,
            head_shape,
            jax.ShapeDtypeStruct((b, t, POOL_WIDTH), F32),
            jax.ShapeDtypeStruct((b, t, 2 * d), BF16),
        ],
        compiler_params=_params("parallel", "parallel"),
        name="proj",
    )(x, ln_g.reshape(1, d), ln_b.reshape(1, d), w_in_b)


def _attn_kernel(q_ref, k_ref, v_ref, tri_ref, o_ref, acc_ref, carry_ref, qm_ref, *, tq, q_pos0):
    tk = KEY_TILE
    pairs = SB_HEADS // 2
    q_first = q_pos0 + pl.program_id(1) * tq
    low_half = lax.broadcasted_iota(I32, (1, 2 * SB_HEAD_DIM), 1) < SB_HEAD_DIM

    def split_heads(x2):
        zero = jnp.zeros_like(x2)
        return jnp.concatenate([jnp.where(low_half, x2, zero), jnp.where(low_half, zero, x2)], axis=0)

    acc_ref[...] = jnp.zeros_like(acc_ref)
    carry_ref[...] = jnp.zeros_like(carry_ref)
    for pair in range(pairs):
        qm_ref[pair] = split_heads(q_ref[0, :, pair * 2 * SB_HEAD_DIM:(pair + 1) * 2 * SB_HEAD_DIM])

    def step(j, masked):
        k0 = pl.multiple_of(j * tk, tk)
        k_tile = k_ref[0, pl.ds(k0, tk), :]
        v_tile = v_ref[0, pl.ds(k0, tk), :]
        s = jnp.concatenate(
            [lax.dot_general(qm_ref[pair], k_tile[:, pair * 2 * SB_HEAD_DIM:(pair + 1) * 2 * SB_HEAD_DIM], _NT,
                             preferred_element_type=F32) for pair in range(pairs)],
            axis=0).reshape(SB_HEADS, tq, tk)
        soft = jnp.log(1.0 + jnp.exp(-jnp.abs(s)))
        log_rest = -(jnp.maximum(s, 0.0) + soft)
        log_beta = log_rest + s
        if masked:
            mask = ((k0 + lax.broadcasted_iota(I32, (tq, tk), 1))
                    < (q_first + lax.broadcasted_iota(I32, (tq, tk), 0)))[None]
            log_rest = jnp.where(mask, log_rest, 0.0)
        hi = log_rest.astype(BF16)
        lo = (log_rest - hi.astype(F32)).astype(BF16)
        sums = _dot(jnp.concatenate([hi, lo], axis=-1).reshape(SB_HEADS * tq, 2 * tk), tri_ref[...])
        sums = sums.reshape(SB_HEADS, tq, 2 * tk)
        carry = carry_ref[...]
        a = jnp.exp(log_beta + sums[:, :, :tk] + carry)
        if masked:
            a = jnp.where(mask, a, 0.0)
        a = a.astype(BF16)
        carry = carry + sums[:, :, tk:]
        carry_ref[...] = carry
        for pair in range(pairs):
            lanes = slice(pair * 2 * SB_HEAD_DIM, (pair + 1) * 2 * SB_HEAD_DIM)
            a2 = jnp.concatenate([a[2 * pair], a[2 * pair + 1]], axis=1)
            acc_ref[:, lanes] += _dot(a2, split_heads(v_tile[:, lanes]))
        return (jnp.max(carry) > -EXP_UNDERFLOW).astype(I32)

    j_first = (q_first + tq - 1) // tk
    go = step(j_first, True)
    lax.while_loop(lambda st: jnp.logical_and(st[0] >= 0, st[1] > 0),
                   lambda st: (st[0] - 1, step(st[0], False)), (j_first - 1, go))
    o_ref[0] = acc_ref[...].astype(BF16)


def _attn(qb, kb, vb, *, q_pos0):
    b, t, w = qb.shape
    s_k = kb.shape[1]
    tq = min(MAX_QUERY_TILE, t)
    tk = KEY_TILE
    assert t % tq == 0 and s_k % tk == 0 and s_k >= q_pos0 + t
    assert tk % tq == 0 and q_pos0 % tq == 0
    r = lax.broadcasted_iota(I32, (2 * tk, 2 * tk), 0) % tk
    c = lax.broadcasted_iota(I32, (2 * tk, 2 * tk), 1)
    tri = jnp.logical_or(r > c, c >= tk).astype(BF16)
    return pl.pallas_call(
        functools.partial(_attn_kernel, tq=tq, q_pos0=q_pos0),
        grid=(b, t // tq),
        in_specs=[
            pl.BlockSpec((1, tq, w), lambda i, j: (i, j, 0)),
            pl.BlockSpec((1, s_k, w), lambda i, j: (i, 0, 0)),
            pl.BlockSpec((1, s_k, w), lambda i, j: (i, 0, 0)),
            pl.BlockSpec((2 * tk, 2 * tk), lambda i, j: (0, 0)),
        ],
        out_specs=pl.BlockSpec((1, tq, w), lambda i, j: (i, j, 0)),
        out_shape=jax.ShapeDtypeStruct((b, t, w), BF16),
        scratch_shapes=[pltpu.VMEM((tq, w), F32), pltpu.VMEM((SB_HEADS, tq, tk), F32),
                        pltpu.VMEM((SB_HEADS // 2, 2 * tq, 2 * SB_HEAD_DIM), BF16)],
        compiler_params=_params("parallel", "parallel"),
        name="attn",
    )(qb, kb, vb, tri)


def _merge_kernel(x_ref, osb_ref, u_ref, uprev_ref, hist_ref, sg_ref, g0_ref, b0_ref, wsb_ref, wgrp_ref,
                  pscale_ref, wpool_ref, wo_ref, g1_ref, b1_ref, h_ref, hrows_ref, ext_ref, *, tm, hist_valid):
    t = pl.program_id(1)
    h0 = _layer_norm(x_ref[0], g0_ref[...], b0_ref[...])
    u = u_ref[0]
    ext_ref[:HALO, :] = jnp.where(t == 0, hist_ref[0], uprev_ref[0])
    ext_ref[HALO:, :] = u
    rows_seen = t * tm + lax.broadcasted_iota(I32, (tm, 1), 0) + (1 + hist_valid)
    pool_out = []
    for g, window in enumerate(POOL_WINDOWS):
        lanes = slice(g * POOL_GROUP_WIDTH, (g + 1) * POOL_GROUP_WIDTH)
        total = u[:, lanes]
        for back in range(1, window):
            total = total + ext_ref[HALO - back:HALO - back + tm, lanes]
        count = jnp.minimum(rows_seen, window).astype(F32)
        pooled = total / count - u[:, lanes]
        pool_out.append(_dot(pooled.astype(BF16), wgrp_ref[g]))
    o_pool = jnp.concatenate(pool_out, axis=1) * pscale_ref[...]
    a = _dot(osb_ref[0], wsb_ref[...])
    p = _dot(o_pool.astype(BF16), wpool_ref[...])
    sg = sg_ref[0].astype(F32)
    m = sg[:, :D_MODEL] * a + sg[:, D_MODEL:] * p
    mix = _dot(m.astype(BF16), wo_ref[...])
    h1 = _layer_norm(DEEPNORM_ALPHA * h0 + mix, g1_ref[...], b1_ref[...])
    h_ref[...] = h1
    hrows_ref[...] = _to_tiles(h1)


def _merge(x, osb, u, hist, sg, ln0_g, ln0_b, wsb, wgrp, pscale, wpool, wo, ln1_g, ln1_b, *, hist_valid):
    b, t, d = x.shape
    tm = min(TOKEN_TILE, t)
    nt = t // tm
    row = lambda width: pl.BlockSpec((1, tm, width), lambda i, j: (i, j, 0))
    const = lambda shape: pl.BlockSpec(shape, lambda i, j: (0,) * len(shape))
    prev = pl.BlockSpec((1, HALO, POOL_WIDTH), lambda i, j: (i, jnp.maximum(j * (tm // HALO) - 1, 0), 0))
    return pl.pallas_call(
        functools.partial(_merge_kernel, tm=tm, hist_valid=hist_valid),
        grid=(b, nt),
        in_specs=[
            row(d), row(SB_WIDTH), row(POOL_WIDTH), prev,
            pl.BlockSpec((1, HALO, POOL_WIDTH), lambda i, j: (i, 0, 0)),
            row(2 * d), const((1, d)), const((1, d)), const((SB_WIDTH, d)),
            const((len(POOL_WINDOWS), POOL_GROUP_WIDTH, POOL_GROUP_WIDTH)), const((1, POOL_WIDTH)),
            const((POOL_WIDTH, d)), const((d, d)), const((1, d)), const((1, d)),
        ],
        out_specs=[pl.BlockSpec((tm, d), lambda i, j: (i * nt + j, 0)),
                   pl.BlockSpec(_tiles_shape(tm), lambda i, j: (i * nt + j, 0, 0))],
        out_shape=[jax.ShapeDtypeStruct((b * t, d), F32), jax.ShapeDtypeStruct(_tiles_shape(b * t), BF16)],
        scratch_shapes=[pltpu.VMEM((HALO + tm, POOL_WIDTH), F32)],
        compiler_params=_params("parallel", "parallel"),
        name="merge",
    )(x, osb, u, u, hist, sg, ln0_g.reshape(1, d), ln0_b.reshape(1, d), wsb, wgrp, pscale.reshape(1, -1),
      wpool, wo, ln1_g.reshape(1, d), ln1_b.reshape(1, d))


def _first_max(vals, index, n):
    top = jnp.max(vals, axis=0, keepdims=True)
    first = jnp.min(jnp.where(vals == top, index, n), axis=0, keepdims=True)
    return top, first


def _route_kernel(h_ref, wr_ref, bias_ref, tri_ref, cin_ref, idx_ref, gate_ref, rank_ref, cout_ref, *, tm):
    @pl.when(pl.program_id(0) == 0)
    def _():
        cout_ref[...] = cin_ref[...]

    h = h_ref[...].astype(BF16)
    scores = _sigmoid(lax.dot_general(wr_ref[...], h, _NT, preferred_element_type=F32))
    biased = scores + bias_ref[...]
    neg = jnp.float32(-jnp.inf)

    in_group = lax.broadcasted_iota(I32, (GROUP_SIZE, tm), 0)
    group_scores = []
    for g in range(N_EXPERT_GROUPS):
        blk = biased[g * GROUP_SIZE:(g + 1) * GROUP_SIZE, :]
        top, first = _first_max(blk, in_group, GROUP_SIZE)
        second = jnp.max(jnp.where(in_group == first, neg, blk), axis=0, keepdims=True)
        group_scores.append(top + second)
    remaining = jnp.concatenate(group_scores, axis=0)
    group_id = lax.broadcasted_iota(I32, (N_EXPERT_GROUPS, tm), 0)
    kept = jnp.zeros((N_EXPERT_GROUPS, tm), F32)
    for _ in range(TOPK_GROUPS):
        _, first = _first_max(remaining, group_id, N_EXPERT_GROUPS)
        hit = group_id == first
        kept = jnp.where(hit, 1.0, kept)
        remaining = jnp.where(hit, neg, remaining)
    remaining = jnp.concatenate(
        [jnp.where(kept[g:g + 1, :] > 0.5, biased[g * GROUP_SIZE:(g + 1) * GROUP_SIZE, :], neg)
         for g in range(N_EXPERT_GROUPS)], axis=0)

    expert_id = lax.broadcasted_iota(I32, (N_EXPERTS, tm), 0)
    chosen, picks, picked_scores = jnp.zeros((N_EXPERTS, tm), F32), [], []
    for _ in range(TOP_K):
        _, first = _first_max(remaining, expert_id, N_EXPERTS)
        hit = expert_id == first
        picks.append((first, hit))
        picked_scores.append(jnp.sum(jnp.where(hit, scores, 0.0), axis=0, keepdims=True))
        chosen = jnp.where(hit, 1.0, chosen)
        remaining = jnp.where(hit, neg, remaining)
    denom = picked_scores[0]
    for sc in picked_scores[1:]:
        denom = denom + sc
    gate_ref[...] = jnp.concatenate([sc / denom * ROUTED_SCALE for sc in picked_scores], axis=0)
    idx_ref[...] = jnp.concatenate([first for first, _ in picks], axis=0)

    slot = cout_ref[...] + _dot(chosen.astype(BF16), tri_ref[...])
    rank_ref[...] = jnp.concatenate(
        [jnp.sum(jnp.where(hit, slot, 0.0), axis=0, keepdims=True) for _, hit in picks], axis=0).astype(I32)
    cout_ref[...] += jnp.sum(chosen, axis=1, keepdims=True)


def _route(h, wr_t, bias, count_in):
    n = h.shape[0]
    tm = min(TOKEN_TILE, n)
    assert n % tm == 0
    earlier = (lax.broadcasted_iota(I32, (tm, tm), 0) < lax.broadcasted_iota(I32, (tm, tm), 1)).astype(BF16)
    const = lambda shape: pl.BlockSpec(shape, lambda i: (0,) * len(shape))
    picks = pl.BlockSpec((TOP_K, tm), lambda i: (0, i))
    return pl.pallas_call(
        functools.partial(_route_kernel, tm=tm),
        grid=(n // tm,),
        in_specs=[pl.BlockSpec((tm, D_MODEL), lambda i: (i, 0)), const((N_EXPERTS, D_MODEL)),
                  const((N_EXPERTS, 1)), const((tm, tm)), const((N_EXPERTS, 1))],
        out_specs=[picks, picks, picks, const((N_EXPERTS, 1))],
        out_shape=[jax.ShapeDtypeStruct((TOP_K, n), I32), jax.ShapeDtypeStruct((TOP_K, n), F32),
                   jax.ShapeDtypeStruct((TOP_K, n), I32), jax.ShapeDtypeStruct((N_EXPERTS, 1), F32)],
        compiler_params=_params("arbitrary"),
        name="route",
    )(h, wr_t, bias, earlier, count_in)


def _slot_kernel(idx_ref, rank_ref, start_ref, dest_ref, *, tm):
    expert_id = lax.broadcasted_iota(I32, (N_EXPERTS, tm), 0)
    start = start_ref[...]
    rows = [jnp.sum(jnp.where(expert_id == idx_ref[j:j + 1, :], start, 0.0), axis=0, keepdims=True)
            for j in range(TOP_K)]
    dest_ref[...] = jnp.concatenate(rows, axis=0).astype(I32) + rank_ref[...]


def _slots(idx, rank, start):
    n = idx.shape[1]
    tm = min(TOKEN_TILE, n)
    assert n % tm == 0
    picks = pl.BlockSpec((TOP_K, tm), lambda i: (0, i))
    return pl.pallas_call(
        functools.partial(_slot_kernel, tm=tm),
        grid=(n // tm,),
        in_specs=[picks, picks, pl.BlockSpec((N_EXPERTS, 1), lambda i: (0, 0))],
        out_specs=picks,
        out_shape=jax.ShapeDtypeStruct((TOP_K, n), I32),
        compiler_params=_params("parallel"),
        name="slots",
    )(idx, rank, start)


def _start_row_copies(tm, copy_of):
    def group(g, carry):
        for r in range(SUBLANES):
            for j in range(TOP_K):
                pair = g * (SUBLANES * TOP_K) + (r * TOP_K + j)
                copy_of(g * (SUBLANES // 2) + r // 2, (r % 2) * SUBLANES, j, pair).start(priority=(r + j) % 2)
        return carry

    lax.fori_loop(0, tm // SUBLANES, group, 0)


def _dispatch_kernel(dtile_ref, dsub_ref, h_ref, *rest, tm):
    xs_ref, sem = rest[-2], rest[-1]
    _start_row_copies(tm, lambda tile, sub, j, pair: pltpu.make_async_copy(
        _tile_row(h_ref, tile, sub), _tile_row(xs_ref, dtile_ref[pair], dsub_ref[pair]), sem))
    for _ in range(TOP_K):
        pltpu.make_async_copy(h_ref, xs_ref.at[pl.ds(0, tm // 2)], sem).wait()


def _dispatch(dest, h, xs, cap):
    n = 2 * h.shape[0]
    tm = min(GATHER_TILE, n)
    assert n % tm == 0
    slot_list = pl.BlockSpec((tm * TOP_K,), lambda i: (i,), memory_space=pltpu.SMEM)
    in_specs = [slot_list, slot_list, pl.BlockSpec(_tiles_shape(tm), lambda i: (i, 0, 0))]
    args = [*dest, h]
    aliases = {}
    if xs is not None:
        in_specs.append(pl.BlockSpec(memory_space=pl.ANY))
        args.append(xs)
        aliases = {3: 0}
    return pl.pallas_call(
        functools.partial(_dispatch_kernel, tm=tm),
        grid=(n // tm,),
        in_specs=in_specs,
        out_specs=pl.BlockSpec(memory_space=pl.ANY),
        out_shape=jax.ShapeDtypeStruct(_tiles_shape(cap), BF16),
        scratch_shapes=[pltpu.SemaphoreType.DMA],
        input_output_aliases=aliases,
        compiler_params=pltpu.CompilerParams(dimension_semantics=("arbitrary",), vmem_limit_bytes=VMEM_LIMIT,
                                             has_side_effects=True),
        name="dispatch",
    )(*args)


def _expert_kernel(first_ref, nblk_ref, used_ref, wg_ref, wu_ref, wd_ref, xs_ref, ys_ref,
                   wg_b, wu_b, wd_b, x_buf, y_buf, x_sem, y_sem):
    e = pl.program_id(0)
    blk, nx, ny, ahead = EXPERT_BLOCK, EXPERT_X_SLOTS, EXPERT_Y_SLOTS, EXPERT_X_AHEAD
    first, n, used = first_ref[e], nblk_ref[e], used_ref[0]

    tiles = blk // 2

    def x_copy(g):
        slot = g % nx
        return pltpu.make_async_copy(xs_ref.at[pl.ds(pl.multiple_of(g * tiles, tiles), tiles)], x_buf.at[slot],
                                     x_sem.at[slot])

    def y_copy(g):
        slot = g % ny
        return pltpu.make_async_copy(y_buf.at[slot], ys_ref.at[pl.ds(pl.multiple_of(g * tiles, tiles), tiles)],
                                     y_sem.at[slot])

    @pl.when(e == 0)
    def _():
        for g in range(ahead):
            @pl.when(g < used)
            def _():
                x_copy(g).start()

    def compute(blocks):
        for g in blocks:
            @pl.when(g + ahead < used)
            def _():
                x_copy(g + ahead).start()
        for g in blocks:
            x_copy(g).wait()
        x = jnp.concatenate([_from_tiles(x_buf[g % nx]) for g in blocks], axis=0)
        gate = _dot(x, wg_b[...])
        hid = gate * _sigmoid(gate) * _dot(x, wu_b[...])
        y = _dot(hid.astype(BF16), wd_b[...])
        for k, g in enumerate(blocks):
            @pl.when(g >= ny)
            def _():
                y_copy(g - ny).wait()

            y_buf[g % ny] = _to_tiles(y[k * blk:(k + 1) * blk])
            y_copy(g).start()

    @pl.when(n > 0)
    def _():
        wg_b[...] = wg_ref[0].astype(BF16)
        wu_b[...] = wu_ref[0].astype(BF16)
        wd_b[...] = wd_ref[0].astype(BF16)

        def pair(p, carry):
            compute([first + 2 * p, first + 2 * p + 1])
            return carry

        lax.fori_loop(0, n // 2, pair, 0)

        @pl.when(n % 2 == 1)
        def _():
            compute([first + n - 1])

    @pl.when(e == pl.num_programs(0) - 1)
    def _():
        for back in range(ny, 0, -1):
            @pl.when(used >= back)
            def _():
                y_copy(used - back).wait()


def _experts(first, nblk, used, xs, w_gate, w_up, w_down):
    cap = 2 * xs.shape[0]
    blk = EXPERT_BLOCK
    gate_up = pl.BlockSpec((1, D_MODEL, EXPERT_DIM), lambda e, *_: (e, 0, 0))
    return pl.pallas_call(
        _expert_kernel,
        grid_spec=pltpu.PrefetchScalarGridSpec(
            num_scalar_prefetch=3,
            grid=(N_EXPERTS,),
            in_specs=[gate_up, gate_up, pl.BlockSpec((1, EXPERT_DIM, D_MODEL), lambda e, *_: (e, 0, 0)),
                      pl.BlockSpec(memory_space=pl.ANY)],
            out_specs=pl.BlockSpec(memory_space=pl.ANY),
            scratch_shapes=[pltpu.VMEM((D_MODEL, EXPERT_DIM), BF16), pltpu.VMEM((D_MODEL, EXPERT_DIM), BF16),
                            pltpu.VMEM((EXPERT_DIM, D_MODEL), BF16),
                            pltpu.VMEM((EXPERT_X_SLOTS,) + _tiles_shape(blk), BF16),
                            pltpu.VMEM((EXPERT_Y_SLOTS,) + _tiles_shape(blk), BF16),
                            pltpu.SemaphoreType.DMA((EXPERT_X_SLOTS,)), pltpu.SemaphoreType.DMA((EXPERT_Y_SLOTS,))],
        ),
        out_shape=jax.ShapeDtypeStruct(_tiles_shape(cap), BF16),
        compiler_params=_params("arbitrary"),
        name="experts",
    )(first, nblk, used, w_gate, w_up, w_down, xs)


def _combine_kernel(dtile0_ref, dsub0_ref, dtile1_ref, dsub1_ref, h_ref, gate_ref, ys_ref, wsg_ref, wsu_ref, wsd_ref,
                    g_ref, b_ref, o_ref, rows_ref, sem, *, tm):
    i = pl.program_id(0)
    slot = i % 2

    def gather(dtile_ref, dsub_ref, into):
        _start_row_copies(tm, lambda tile, sub, j, pair: pltpu.make_async_copy(
            _tile_row(ys_ref, dtile_ref[pair], dsub_ref[pair]), _tile_row(rows_ref.at[into, j], tile, sub),
            sem.at[into]))

    @pl.when(i == 0)
    def _():
        gather(dtile0_ref, dsub0_ref, 0)

    @pl.when(i + 1 < pl.num_programs(0))
    def _():
        gather(dtile1_ref, dsub1_ref, 1 - slot)

    h = h_ref[...]
    hb = h.astype(BF16)
    gate = _dot(hb, wsg_ref[...])
    hid = gate * _sigmoid(gate) * _dot(hb, wsu_ref[...])
    ffn = _dot(hid.astype(BF16), wsd_ref[...])
    for j in range(TOP_K):
        pltpu.make_async_copy(ys_ref.at[pl.ds(0, tm // 2)], rows_ref.at[slot, j], sem.at[slot]).wait()
    weights = gate_ref[...]
    for j in range(TOP_K):
        ffn = ffn + weights[:, j:j + 1] * _from_tiles(rows_ref[slot, j]).astype(F32)
    o_ref[...] = _layer_norm(DEEPNORM_ALPHA * h + ffn, g_ref[...], b_ref[...])


def _combine(dest, h, gate_t, ys, wsg, wsu, wsd, ln_g, ln_b):
    n, d = h.shape[0], D_MODEL
    tm = min(GATHER_TILE, n)
    assert n % tm == 0
    const = lambda shape: pl.BlockSpec(shape, lambda i: (0,) * len(shape))
    steps = n // tm
    first_list = pl.BlockSpec((tm * TOP_K,), lambda i: (0,), memory_space=pltpu.SMEM)
    next_list = pl.BlockSpec((tm * TOP_K,), lambda i: (jnp.minimum(i + 1, steps - 1),), memory_space=pltpu.SMEM)
    return pl.pallas_call(
        functools.partial(_combine_kernel, tm=tm),
        grid=(steps,),
        in_specs=[first_list, first_list, next_list, next_list,
                  pl.BlockSpec((tm, d), lambda i: (i, 0)),
                  pl.BlockSpec((tm, TOP_K), lambda i: (i, 0)),
                  pl.BlockSpec(memory_space=pl.ANY),
                  const((d, EXPERT_DIM)), const((d, EXPERT_DIM)), const((EXPERT_DIM, d)),
                  const((1, d)), const((1, d))],
        out_specs=pl.BlockSpec((tm, d), lambda i: (i, 0)),
        out_shape=jax.ShapeDtypeStruct((n, d), F32),
        scratch_shapes=[pltpu.VMEM((2, TOP_K) + _tiles_shape(tm), BF16), pltpu.SemaphoreType.DMA((2,))],
        compiler_params=_params("arbitrary"),
        name="combine",
    )(*dest, *dest, h, gate_t, ys, wsg, wsu, wsd, ln_g.reshape(1, d), ln_b.reshape(1, d))


def _key_major(cache):
    b, h, s, hd = cache.shape
    return cache.transpose(0, 2, 1, 3).reshape(b, s, h * hd).astype(BF16)


def kernel(x_prompt, x_sample, cache_k, cache_v, state_pool, ln0_g, ln0_b, w_in, w_sb_up, w_pool_grp, pool_scale,
           w_pool_up, w_o, ln1_g, ln1_b, w_router, router_bias, w_exp_gate, w_exp_up, w_exp_down, w_sh_gate,
           w_sh_up, w_sh_down, ln2_g, ln2_b):
    assert w_in.shape[0] == DEPTH
    b_p, t_p, d = x_prompt.shape
    b_s, t_s, _ = x_sample.shape
    past = cache_k.shape[3]
    n_p, n_s = b_p * t_p, b_s * t_s
    layer = 0
    w_in_b = w_in[layer].astype(BF16)
    mix_w = (w_sb_up[layer].astype(BF16), w_pool_grp[layer].astype(BF16), pool_scale[layer],
             w_pool_up[layer].astype(BF16), w_o[layer].astype(BF16), ln1_g[layer], ln1_b[layer])

    qb, kb, vb, k_p, v_p, u_p, sg_p = _proj(x_prompt, ln0_g, ln0_b, w_in_b)
    osb_p = _attn(qb, kb, vb, q_pos0=0)
    hist_p = jnp.zeros((b_p, HALO, POOL_WIDTH), F32)
    h_p, hrows_p = _merge(x_prompt, osb_p, u_p, hist_p, sg_p, ln0_g, ln0_b, *mix_w, hist_valid=0)

    qs, ks, vs, k_s, v_s, u_s, sg_s = _proj(x_sample, ln0_g, ln0_b, w_in_b)
    key_rows = -(-(past + t_s) // KEY_TILE) * KEY_TILE
    pad = ((0, 0), (0, key_rows - past - t_s), (0, 0))
    k_all = jnp.pad(jnp.concatenate([_key_major(cache_k[layer]), ks], axis=1), pad)
    v_all = jnp.pad(jnp.concatenate([_key_major(cache_v[layer]), vs], axis=1), pad)
    osb_s = _attn(qs, k_all, v_all, q_pos0=past)
    hist_s = jnp.concatenate([jnp.zeros((b_s, 1, POOL_WIDTH), F32), state_pool[layer]], axis=1)
    h_s, hrows_s = _merge(x_sample, osb_s, u_s, hist_s, sg_s, ln0_g, ln0_b, *mix_w, hist_valid=POOL_STATE)

    wr_t = w_router[layer].T.astype(BF16)
    bias = router_bias[layer].astype(F32).reshape(N_EXPERTS, 1)
    idx_p, gate_p, rank_p, count = _route(h_p, wr_t, bias, jnp.zeros((N_EXPERTS, 1), F32))
    idx_s, gate_s, rank_s, count = _route(h_s, wr_t, bias, count)

    blk = EXPERT_BLOCK
    n_blocks = ((n_p + n_s) * TOP_K + N_EXPERTS * (blk - 1) + blk - 1) // blk
    cap = n_blocks * blk
    assert cap < 2 ** 24
    nblk = (count[:, 0].astype(I32) + blk - 1) // blk
    first = jnp.cumsum(nblk) - nblk
    used = jnp.sum(nblk).reshape(1)
    start = (first * blk).astype(F32).reshape(N_EXPERTS, 1)
    def slot_lists(idx, rank):
        dest = _slots(idx, rank, start).T.reshape(-1)
        return dest // 2, dest % 2 * SUBLANES

    dest_p = slot_lists(idx_p, rank_p)
    dest_s = slot_lists(idx_s, rank_s)

    xs = _dispatch(dest_p, hrows_p, None, cap)
    xs = _dispatch(dest_s, hrows_s, xs, cap)
    ys = _experts(first, nblk, used, xs, w_exp_gate[layer], w_exp_up[layer], w_exp_down[layer])

    shared_w = (w_sh_gate[layer].astype(BF16), w_sh_up[layer].astype(BF16), w_sh_down[layer].astype(BF16),
                ln2_g[layer], ln2_b[layer])
    y_p = _combine(dest_p, h_p, gate_p.T, ys, *shared_w)
    y_s = _combine(dest_s, h_s, gate_s.T, ys, *shared_w)

    new_pool_p = u_p[:, t_p - POOL_STATE:][None]
    new_pool_s = jnp.concatenate([state_pool[layer], u_s], axis=1)[:, -POOL_STATE:][None]
    return (y_p.reshape(b_p, t_p, d), y_s.reshape(b_s, t_s, d), k_p, v_p, new_pool_p, k_s, v_s, new_pool_s)
```

```python
import functools

import jax
import jax.numpy as jnp
from jax import lax
from jax.experimental import pallas as pl
from jax.experimental.pallas import tpu as pltpu

F32 = jnp.float32
BF16 = jnp.bfloat16
I32 = jnp.int32

D_MODEL = 1024
SB_HEADS = 8
SB_HEAD_DIM = 64
SB_WIDTH = SB_HEADS * SB_HEAD_DIM
SB_SCALE = SB_HEAD_DIM ** -0.5
POOL_WINDOWS = (2, 4, 8, 16)
POOL_WIDTH = D_MODEL // 2
POOL_GROUP_WIDTH = POOL_WIDTH // len(POOL_WINDOWS)
POOL_STATE = max(POOL_WINDOWS) - 1
HALO = POOL_STATE + 1
N_EXPERTS = 256
TOP_K = 8
N_EXPERT_GROUPS = 8
TOPK_GROUPS = 4
GROUP_SIZE = N_EXPERTS // N_EXPERT_GROUPS
EXPERT_DIM = D_MODEL // 4
ROUTED_SCALE = 2.5
LN_EPS = 1e-5
DEPTH = 1
DEEPNORM_ALPHA = (2 * DEPTH) ** 0.25

SUBLANES, LANES = 8, 128
ROW_TILE = (SUBLANES, LANES)
assert SUBLANES * LANES == D_MODEL
EXPERT_X_AHEAD = 5
EXPERT_X_SLOTS = EXPERT_X_AHEAD + 2
EXPERT_Y_SLOTS = 4
KEY_TILE = 128
MAX_QUERY_TILE = 128
EXPERT_BLOCK = 256
TOKEN_TILE = 512
GATHER_TILE = 256
EXP_UNDERFLOW = 104.0
VMEM_LIMIT = 56 * 1024 * 1024

_NT = (((1,), (1,)), ((), ()))


def _params(*semantics):
    return pltpu.CompilerParams(dimension_semantics=semantics, vmem_limit_bytes=VMEM_LIMIT)


def _layer_norm(x, g, b):
    mu = jnp.mean(x, axis=-1, keepdims=True)
    xc = x - mu
    var = jnp.mean(xc * xc, axis=-1, keepdims=True)
    return xc * lax.rsqrt(var + LN_EPS) * g + b


def _sigmoid(x):
    return 1.0 / (1.0 + jnp.exp(-x))


def _dot(a, b):
    return jnp.dot(a, b, preferred_element_type=F32)


def _to_tiles(x):
    return x.reshape((x.shape[0],) + ROW_TILE)


def _from_tiles(x):
    return x.reshape(x.shape[0], D_MODEL)


def _proj_kernel(x_ref, g_ref, b_ref, w_ref, qb_ref, kb_ref, vb_ref, k_ref, v_ref, u_ref, sg_ref):
    h = _layer_norm(x_ref[0], g_ref[...], b_ref[...]).astype(BF16)

    def proj(lo, hi):
        return _dot(h, w_ref[:, lo:hi])

    qb_ref[0] = (proj(0, SB_WIDTH) * SB_SCALE).astype(BF16)
    zk = proj(SB_WIDTH, 2 * SB_WIDTH)
    kb_ref[0] = zk.astype(BF16)
    zv = proj(2 * SB_WIDTH, 3 * SB_WIDTH)
    vb_ref[0] = zv.astype(BF16)
    for hd in range(SB_HEADS):
        sl = slice(hd * SB_HEAD_DIM, (hd + 1) * SB_HEAD_DIM)
        k_ref[0, 0, hd] = zk[:, sl]
        v_ref[0, 0, hd] = zv[:, sl]
    u0 = 3 * SB_WIDTH
    u_ref[0] = proj(u0, u0 + POOL_WIDTH)
    g0 = u0 + POOL_WIDTH
    for c in range(2 * D_MODEL // SB_WIDTH):
        sl = slice(c * SB_WIDTH, (c + 1) * SB_WIDTH)
        sg_ref[0, :, sl] = _sigmoid(proj(g0 + c * SB_WIDTH, g0 + (c + 1) * SB_WIDTH)).astype(BF16)


def _proj(x, ln_g, ln_b, w_in_b):
    b, t, d = x.shape
    tm = min(TOKEN_TILE, t)
    in_width = w_in_b.shape[1]
    row = lambda width: pl.BlockSpec((1, tm, width), lambda i, j: (i, j, 0))
    heads = pl.BlockSpec((1, 1, SB_HEADS, tm, SB_HEAD_DIM), lambda i, j: (0, i, 0, j, 0))
    const = lambda shape: pl.BlockSpec(shape, lambda i, j: (0,) * len(shape))
    head_shape = jax.ShapeDtypeStruct((1, b, SB_HEADS, t, SB_HEAD_DIM), F32)
    return pl.pallas_call(
        _proj_kernel,
        grid=(b, t // tm),
        in_specs=[row(d), const((1, d)), const((1, d)), const((d, in_width))],
        out_specs=[row(SB_WIDTH), row(SB_WIDTH), row(SB_WIDTH), heads, heads, row(POOL_WIDTH), row(2 * d)],
        out_shape=[
            jax.ShapeDtypeStruct((b, t, SB_WIDTH), BF16),
            jax.ShapeDtypeStruct((b, t, SB_WIDTH), BF16),
            jax.ShapeDtypeStruct((b, t, SB_WIDTH), BF16),
            head_shape,
            head_shape,
            jax.ShapeDtypeStruct((b, t, POOL_WIDTH), F32),
            jax.ShapeDtypeStruct((b, t, 2 * d), BF16),
        ],
        compiler_params=_params("parallel", "parallel"),
        name="proj",
    )(x, ln_g.reshape(1, d), ln_b.reshape(1, d), w_in_b)


def _attn_kernel(q_ref, k_ref, v_ref, tri_ref, o_ref, acc_ref, carry_ref, qm_ref, *, tq, q_pos0):
    tk = KEY_TILE
    pairs = SB_HEADS // 2
    q_first = q_pos0 + pl.program_id(1) * tq
    low_half = lax.broadcasted_iota(I32, (1, 2 * SB_HEAD_DIM), 1) < SB_HEAD_DIM

    def split_heads(x2):
        zero = jnp.zeros_like(x2)
        return jnp.concatenate([jnp.where(low_half, x2, zero), jnp.where(low_half, zero, x2)], axis=0)

    acc_ref[...] = jnp.zeros_like(acc_ref)
    carry_ref[...] = jnp.zeros_like(carry_ref)
    for pair in range(pairs):
        qm_ref[pair] = split_heads(q_ref[0, :, pair * 2 * SB_HEAD_DIM:(pair + 1) * 2 * SB_HEAD_DIM])

    def step(j, masked):
        k0 = pl.multiple_of(j * tk, tk)
        k_tile = k_ref[0, pl.ds(k0, tk), :]
        v_tile = v_ref[0, pl.ds(k0, tk), :]
        s = jnp.concatenate(
            [lax.dot_general(qm_ref[pair], k_tile[:, pair * 2 * SB_HEAD_DIM:(pair + 1) * 2 * SB_HEAD_DIM], _NT,
                             preferred_element_type=F32) for pair in range(pairs)],
            axis=0).reshape(SB_HEADS, tq, tk)
        soft = jnp.log(1.0 + jnp.exp(-jnp.abs(s)))
        log_rest = -(jnp.maximum(s, 0.0) + soft)
        log_beta = log_rest + s
        if masked:
            mask = ((k0 + lax.broadcasted_iota(I32, (tq, tk), 1))
                    < (q_first + lax.broadcasted_iota(I32, (tq, tk), 0)))[None]
            log_rest = jnp.where(mask, log_rest, 0.0)
        hi = log_rest.astype(BF16)
        lo = (log_rest - hi.astype(F32)).astype(BF16)
        sums = _dot(jnp.concatenate([hi, lo], axis=-1).reshape(SB_HEADS * tq, 2 * tk), tri_ref[...])
        sums = sums.reshape(SB_HEADS, tq, 2 * tk)
        carry = carry_ref[...]
        a = jnp.exp(log_beta + sums[:, :, :tk] + carry)
        if masked:
            a = jnp.where(mask, a, 0.0)
        a = a.astype(BF16)
        carry = carry + sums[:, :, tk:]
        carry_ref[...] = carry
        for pair in range(pairs):
            lanes = slice(pair * 2 * SB_HEAD_DIM, (pair + 1) * 2 * SB_HEAD_DIM)
            a2 = jnp.concatenate([a[2 * pair], a[2 * pair + 1]], axis=1)
            acc_ref[:, lanes] += _dot(a2, split_heads(v_tile[:, lanes]))
        return (jnp.max(carry) > -EXP_UNDERFLOW).astype(I32)

    j_first = (q_first + tq - 1) // tk
    go = step(j_first, True)
    lax.while_loop(lambda st: jnp.logical_and(st[0] >= 0, st[1] > 0),
                   lambda st: (st[0] - 1, step(st[0], False)), (j_first - 1, go))
    o_ref[0] = acc_ref[...].astype(BF16)


def _attn(qb, kb, vb, *, q_pos0):
    b, t, w = qb.shape
    s_k = kb.shape[1]
    tq = min(MAX_QUERY_TILE, t)
    tk = KEY_TILE
    assert t % tq == 0 and s_k % tk == 0 and s_k >= q_pos0 + t
    assert tk % tq == 0 and q_pos0 % tq == 0
    r = lax.broadcasted_iota(I32, (2 * tk, 2 * tk), 0) % tk
    c = lax.broadcasted_iota(I32, (2 * tk, 2 * tk), 1)
    tri = jnp.logical_or(r > c, c >= tk).astype(BF16)
    return pl.pallas_call(
        functools.partial(_attn_kernel, tq=tq, q_pos0=q_pos0),
        grid=(b, t // tq),
        in_specs=[
            pl.BlockSpec((1, tq, w), lambda i, j: (i, j, 0)),
            pl.BlockSpec((1, s_k, w), lambda i, j: (i, 0, 0)),
            pl.BlockSpec((1, s_k, w), lambda i, j: (i, 0, 0)),
            pl.BlockSpec((2 * tk, 2 * tk), lambda i, j: (0, 0)),
        ],
        out_specs=pl.BlockSpec((1, tq, w), lambda i, j: (i, j, 0)),
        out_shape=jax.ShapeDtypeStruct((b, t, w), BF16),
        scratch_shapes=[pltpu.VMEM((tq, w), F32), pltpu.VMEM((SB_HEADS, tq, tk), F32),
                        pltpu.VMEM((SB_HEADS // 2, 2 * tq, 2 * SB_HEAD_DIM), BF16)],
        compiler_params=_params("parallel", "parallel"),
        name="attn",
    )(qb, kb, vb, tri)


def _merge_kernel(x_ref, osb_ref, u_ref, uprev_ref, hist_ref, sg_ref, g0_ref, b0_ref, wsb_ref, wgrp_ref,
                  pscale_ref, wpool_ref, wo_ref, g1_ref, b1_ref, h_ref, ext_ref, *, tm, hist_valid):
    t = pl.program_id(1)
    h0 = _layer_norm(x_ref[0], g0_ref[...], b0_ref[...])
    u = u_ref[0]
    ext_ref[:HALO, :] = jnp.where(t == 0, hist_ref[0], uprev_ref[0])
    ext_ref[HALO:, :] = u
    rows_seen = t * tm + lax.broadcasted_iota(I32, (tm, 1), 0) + (1 + hist_valid)
    pool_out = []
    for g, window in enumerate(POOL_WINDOWS):
        lanes = slice(g * POOL_GROUP_WIDTH, (g + 1) * POOL_GROUP_WIDTH)
        total = u[:, lanes]
        for back in range(1, window):
            total = total + ext_ref[HALO - back:HALO - back + tm, lanes]
        count = jnp.minimum(rows_seen, window).astype(F32)
        pooled = total / count - u[:, lanes]
        pool_out.append(_dot(pooled.astype(BF16), wgrp_ref[g]))
    o_pool = jnp.concatenate(pool_out, axis=1) * pscale_ref[...]
    a = _dot(osb_ref[0], wsb_ref[...])
    p = _dot(o_pool.astype(BF16), wpool_ref[...])
    sg = sg_ref[0].astype(F32)
    m = sg[:, :D_MODEL] * a + sg[:, D_MODEL:] * p
    mix = _dot(m.astype(BF16), wo_ref[...])
    h1 = _layer_norm(DEEPNORM_ALPHA * h0 + mix, g1_ref[...], b1_ref[...])
    h_ref[...] = _to_tiles(h1)


def _merge(x, osb, u, hist, sg, ln0_g, ln0_b, wsb, wgrp, pscale, wpool, wo, ln1_g, ln1_b, *, hist_valid):
    b, t, d = x.shape
    tm = min(TOKEN_TILE, t)
    nt = t // tm
    row = lambda width: pl.BlockSpec((1, tm, width), lambda i, j: (i, j, 0))
    const = lambda shape: pl.BlockSpec(shape, lambda i, j: (0,) * len(shape))
    prev = pl.BlockSpec((1, HALO, POOL_WIDTH), lambda i, j: (i, jnp.maximum(j * (tm // HALO) - 1, 0), 0))
    return pl.pallas_call(
        functools.partial(_merge_kernel, tm=tm, hist_valid=hist_valid),
        grid=(b, nt),
        in_specs=[
            row(d), row(SB_WIDTH), row(POOL_WIDTH), prev,
            pl.BlockSpec((1, HALO, POOL_WIDTH), lambda i, j: (i, 0, 0)),
            row(2 * d), const((1, d)), const((1, d)), const((SB_WIDTH, d)),
            const((len(POOL_WINDOWS), POOL_GROUP_WIDTH, POOL_GROUP_WIDTH)), const((1, POOL_WIDTH)),
            const((POOL_WIDTH, d)), const((d, d)), const((1, d)), const((1, d)),
        ],
        out_specs=pl.BlockSpec((tm,) + ROW_TILE, lambda i, j: (i * nt + j, 0, 0)),
        out_shape=jax.ShapeDtypeStruct((b * t,) + ROW_TILE, F32),
        scratch_shapes=[pltpu.VMEM((HALO + tm, POOL_WIDTH), F32)],
        compiler_params=_params("parallel", "parallel"),
        name="merge",
    )(x, osb, u, u, hist, sg, ln0_g.reshape(1, d), ln0_b.reshape(1, d), wsb, wgrp, pscale.reshape(1, -1),
      wpool, wo, ln1_g.reshape(1, d), ln1_b.reshape(1, d))


def _first_max(vals, index, n):
    top = jnp.max(vals, axis=0, keepdims=True)
    first = jnp.min(jnp.where(vals == top, index, n), axis=0, keepdims=True)
    return top, first


def _route_kernel(h_ref, wr_ref, bias_ref, tri_ref, cin_ref, idx_ref, gate_ref, rank_ref, cout_ref, *, tm):
    @pl.when(pl.program_id(0) == 0)
    def _():
        cout_ref[...] = cin_ref[...]

    h = _from_tiles(h_ref[...]).astype(BF16)
    scores = _sigmoid(lax.dot_general(wr_ref[...], h, _NT, preferred_element_type=F32))
    biased = scores + bias_ref[...]
    neg = jnp.float32(-jnp.inf)

    in_group = lax.broadcasted_iota(I32, (GROUP_SIZE, tm), 0)
    group_scores = []
    for g in range(N_EXPERT_GROUPS):
        blk = biased[g * GROUP_SIZE:(g + 1) * GROUP_SIZE, :]
        top, first = _first_max(blk, in_group, GROUP_SIZE)
        second = jnp.max(jnp.where(in_group == first, neg, blk), axis=0, keepdims=True)
        group_scores.append(top + second)
    remaining = jnp.concatenate(group_scores, axis=0)
    group_id = lax.broadcasted_iota(I32, (N_EXPERT_GROUPS, tm), 0)
    kept = jnp.zeros((N_EXPERT_GROUPS, tm), F32)
    for _ in range(TOPK_GROUPS):
        _, first = _first_max(remaining, group_id, N_EXPERT_GROUPS)
        hit = group_id == first
        kept = jnp.where(hit, 1.0, kept)
        remaining = jnp.where(hit, neg, remaining)
    remaining = jnp.concatenate(
        [jnp.where(kept[g:g + 1, :] > 0.5, biased[g * GROUP_SIZE:(g + 1) * GROUP_SIZE, :], neg)
         for g in range(N_EXPERT_GROUPS)], axis=0)

    expert_id = lax.broadcasted_iota(I32, (N_EXPERTS, tm), 0)
    chosen, picks, picked_scores = jnp.zeros((N_EXPERTS, tm), F32), [], []
    for _ in range(TOP_K):
        _, first = _first_max(remaining, expert_id, N_EXPERTS)
        hit = expert_id == first
        picks.append((first, hit))
        picked_scores.append(jnp.sum(jnp.where(hit, scores, 0.0), axis=0, keepdims=True))
        chosen = jnp.where(hit, 1.0, chosen)
        remaining = jnp.where(hit, neg, remaining)
    denom = picked_scores[0]
    for sc in picked_scores[1:]:
        denom = denom + sc
    gate_ref[...] = jnp.concatenate([sc / denom * ROUTED_SCALE for sc in picked_scores], axis=0)
    idx_ref[...] = jnp.concatenate([first for first, _ in picks], axis=0)

    slot = cout_ref[...] + _dot(chosen.astype(BF16), tri_ref[...])
    rank_ref[...] = jnp.concatenate(
        [jnp.sum(jnp.where(hit, slot, 0.0), axis=0, keepdims=True) for _, hit in picks], axis=0).astype(I32)
    cout_ref[...] += jnp.sum(chosen, axis=1, keepdims=True)


def _route(h, wr_t, bias, count_in):
    n = h.shape[0]
    tm = min(TOKEN_TILE, n)
    assert n % tm == 0
    earlier = (lax.broadcasted_iota(I32, (tm, tm), 0) < lax.broadcasted_iota(I32, (tm, tm), 1)).astype(BF16)
    const = lambda shape: pl.BlockSpec(shape, lambda i: (0,) * len(shape))
    picks = pl.BlockSpec((TOP_K, tm), lambda i: (0, i))
    return pl.pallas_call(
        functools.partial(_route_kernel, tm=tm),
        grid=(n // tm,),
        in_specs=[pl.BlockSpec((tm,) + ROW_TILE, lambda i: (i, 0, 0)), const((N_EXPERTS, D_MODEL)),
                  const((N_EXPERTS, 1)), const((tm, tm)), const((N_EXPERTS, 1))],
        out_specs=[picks, picks, picks, const((N_EXPERTS, 1))],
        out_shape=[jax.ShapeDtypeStruct((TOP_K, n), I32), jax.ShapeDtypeStruct((TOP_K, n), F32),
                   jax.ShapeDtypeStruct((TOP_K, n), I32), jax.ShapeDtypeStruct((N_EXPERTS, 1), F32)],
        compiler_params=_params("arbitrary"),
        name="route",
    )(h, wr_t, bias, earlier, count_in)


def _slot_kernel(idx_ref, rank_ref, start_ref, dest_ref, *, tm):
    expert_id = lax.broadcasted_iota(I32, (N_EXPERTS, tm), 0)
    start = start_ref[...]
    rows = [jnp.sum(jnp.where(expert_id == idx_ref[j:j + 1, :], start, 0.0), axis=0, keepdims=True)
            for j in range(TOP_K)]
    dest_ref[...] = jnp.concatenate(rows, axis=0).astype(I32) + rank_ref[...]


def _slots(idx, rank, start):
    n = idx.shape[1]
    tm = min(TOKEN_TILE, n)
    assert n % tm == 0
    picks = pl.BlockSpec((TOP_K, tm), lambda i: (0, i))
    return pl.pallas_call(
        functools.partial(_slot_kernel, tm=tm),
        grid=(n // tm,),
        in_specs=[picks, picks, pl.BlockSpec((N_EXPERTS, 1), lambda i: (0, 0))],
        out_specs=picks,
        out_shape=jax.ShapeDtypeStruct((TOP_K, n), I32),
        compiler_params=_params("parallel"),
        name="slots",
    )(idx, rank, start)


def _start_row_copies(tm, copy_of):
    def group(g, carry):
        tok0 = pl.multiple_of(g * SUBLANES, SUBLANES)
        for r in range(SUBLANES):
            for j in range(TOP_K):
                copy_of(tok0 + r, j, tok0 * TOP_K + (r * TOP_K + j)).start(priority=(r + j) % 2)
        return carry

    lax.fori_loop(0, tm // SUBLANES, group, 0)


def _dispatch_kernel(dest_ref, h_ref, *rest, tm):
    xs_ref, sem = rest[-2], rest[-1]
    _start_row_copies(
        tm, lambda tok, j, pair: pltpu.make_async_copy(h_ref.at[tok], xs_ref.at[dest_ref[pair]], sem))
    for _ in range(TOP_K):
        pltpu.make_async_copy(h_ref, xs_ref.at[pl.ds(0, tm)], sem).wait()


def _dispatch(dest, h, xs, cap):
    n = h.shape[0]
    tm = min(GATHER_TILE, n)
    assert n % tm == 0
    in_specs = [pl.BlockSpec((tm * TOP_K,), lambda i: (i,), memory_space=pltpu.SMEM),
                pl.BlockSpec((tm,) + ROW_TILE, lambda i: (i, 0, 0))]
    args = [dest, h]
    aliases = {}
    if xs is not None:
        in_specs.append(pl.BlockSpec(memory_space=pl.ANY))
        args.append(xs)
        aliases = {2: 0}
    return pl.pallas_call(
        functools.partial(_dispatch_kernel, tm=tm),
        grid=(n // tm,),
        in_specs=in_specs,
        out_specs=pl.BlockSpec(memory_space=pl.ANY),
        out_shape=jax.ShapeDtypeStruct((cap,) + ROW_TILE, F32),
        scratch_shapes=[pltpu.SemaphoreType.DMA],
        input_output_aliases=aliases,
        compiler_params=pltpu.CompilerParams(dimension_semantics=("arbitrary",), vmem_limit_bytes=VMEM_LIMIT,
                                             has_side_effects=True),
        name="dispatch",
    )(*args)


def _expert_kernel(first_ref, nblk_ref, used_ref, wg_ref, wu_ref, wd_ref, xs_ref, ys_ref,
                   wg_b, wu_b, wd_b, x_buf, y_buf, x_sem, y_sem):
    e = pl.program_id(0)
    blk, nx, ny, ahead = EXPERT_BLOCK, EXPERT_X_SLOTS, EXPERT_Y_SLOTS, EXPERT_X_AHEAD
    first, n, used = first_ref[e], nblk_ref[e], used_ref[0]

    def x_copy(g):
        slot = g % nx
        return pltpu.make_async_copy(xs_ref.at[pl.ds(pl.multiple_of(g * blk, blk), blk)], x_buf.at[slot],
                                     x_sem.at[slot])

    def y_copy(g):
        slot = g % ny
        return pltpu.make_async_copy(y_buf.at[slot], ys_ref.at[pl.ds(pl.multiple_of(g * blk, blk), blk)],
                                     y_sem.at[slot])

    @pl.when(e == 0)
    def _():
        for g in range(ahead):
            @pl.when(g < used)
            def _():
                x_copy(g).start()

    def compute(blocks):
        for g in blocks:
            @pl.when(g + ahead < used)
            def _():
                x_copy(g + ahead).start()
        for g in blocks:
            x_copy(g).wait()
        x = jnp.concatenate([_from_tiles(x_buf[g % nx]) for g in blocks], axis=0).astype(BF16)
        gate = _dot(x, wg_b[...])
        hid = gate * _sigmoid(gate) * _dot(x, wu_b[...])
        y = _dot(hid.astype(BF16), wd_b[...])
        for k, g in enumerate(blocks):
            @pl.when(g >= ny)
            def _():
                y_copy(g - ny).wait()

            y_buf[g % ny] = _to_tiles(y[k * blk:(k + 1) * blk])
            y_copy(g).start()

    @pl.when(n > 0)
    def _():
        wg_b[...] = wg_ref[0].astype(BF16)
        wu_b[...] = wu_ref[0].astype(BF16)
        wd_b[...] = wd_ref[0].astype(BF16)

        def pair(p, carry):
            compute([first + 2 * p, first + 2 * p + 1])
            return carry

        lax.fori_loop(0, n // 2, pair, 0)

        @pl.when(n % 2 == 1)
        def _():
            compute([first + n - 1])

    @pl.when(e == pl.num_programs(0) - 1)
    def _():
        for back in range(ny, 0, -1):
            @pl.when(used >= back)
            def _():
                y_copy(used - back).wait()


def _experts(first, nblk, used, xs, w_gate, w_up, w_down):
    cap = xs.shape[0]
    blk = EXPERT_BLOCK
    gate_up = pl.BlockSpec((1, D_MODEL, EXPERT_DIM), lambda e, *_: (e, 0, 0))
    return pl.pallas_call(
        _expert_kernel,
        grid_spec=pltpu.PrefetchScalarGridSpec(
            num_scalar_prefetch=3,
            grid=(N_EXPERTS,),
            in_specs=[gate_up, gate_up, pl.BlockSpec((1, EXPERT_DIM, D_MODEL), lambda e, *_: (e, 0, 0)),
                      pl.BlockSpec(memory_space=pl.ANY)],
            out_specs=pl.BlockSpec(memory_space=pl.ANY),
            scratch_shapes=[pltpu.VMEM((D_MODEL, EXPERT_DIM), BF16), pltpu.VMEM((D_MODEL, EXPERT_DIM), BF16),
                            pltpu.VMEM((EXPERT_DIM, D_MODEL), BF16),
                            pltpu.VMEM((EXPERT_X_SLOTS, blk) + ROW_TILE, F32),
                            pltpu.VMEM((EXPERT_Y_SLOTS, blk) + ROW_TILE, F32),
                            pltpu.SemaphoreType.DMA((EXPERT_X_SLOTS,)), pltpu.SemaphoreType.DMA((EXPERT_Y_SLOTS,))],
        ),
        out_shape=jax.ShapeDtypeStruct((cap,) + ROW_TILE, F32),
        compiler_params=_params("arbitrary"),
        name="experts",
    )(first, nblk, used, w_gate, w_up, w_down, xs)


def _combine_kernel(dest0_ref, dest1_ref, h_ref, gate_ref, ys_ref, wsg_ref, wsu_ref, wsd_ref, g_ref, b_ref, o_ref,
                    rows_ref, sem, *, tm):
    i = pl.program_id(0)
    slot = i % 2

    def gather(dest_ref, into):
        _start_row_copies(tm, lambda tok, j, pair: pltpu.make_async_copy(
            ys_ref.at[dest_ref[pair]], rows_ref.at[into, j, tok], sem.at[into]))

    @pl.when(i == 0)
    def _():
        gather(dest0_ref, 0)

    @pl.when(i + 1 < pl.num_programs(0))
    def _():
        gather(dest1_ref, 1 - slot)

    h = _from_tiles(h_ref[...])
    hb = h.astype(BF16)
    gate = _dot(hb, wsg_ref[...])
    hid = gate * _sigmoid(gate) * _dot(hb, wsu_ref[...])
    ffn = _dot(hid.astype(BF16), wsd_ref[...])
    for j in range(TOP_K):
        pltpu.make_async_copy(ys_ref.at[pl.ds(0, tm)], rows_ref.at[slot, j], sem.at[slot]).wait()
    weights = gate_ref[...]
    for j in range(TOP_K):
        ffn = ffn + weights[:, j:j + 1] * _from_tiles(rows_ref[slot, j])
    o_ref[...] = _layer_norm(DEEPNORM_ALPHA * h + ffn, g_ref[...], b_ref[...])


def _combine(dest, h, gate_t, ys, wsg, wsu, wsd, ln_g, ln_b):
    n, d = h.shape[0], D_MODEL
    tm = min(GATHER_TILE, n)
    assert n % tm == 0
    const = lambda shape: pl.BlockSpec(shape, lambda i: (0,) * len(shape))
    steps = n // tm
    return pl.pallas_call(
        functools.partial(_combine_kernel, tm=tm),
        grid=(steps,),
        in_specs=[pl.BlockSpec((tm * TOP_K,), lambda i: (0,), memory_space=pltpu.SMEM),
                  pl.BlockSpec((tm * TOP_K,), lambda i: (jnp.minimum(i + 1, steps - 1),), memory_space=pltpu.SMEM),
                  pl.BlockSpec((tm,) + ROW_TILE, lambda i: (i, 0, 0)),
                  pl.BlockSpec((tm, TOP_K), lambda i: (i, 0)),
                  pl.BlockSpec(memory_space=pl.ANY),
                  const((d, EXPERT_DIM)), const((d, EXPERT_DIM)), const((EXPERT_DIM, d)),
                  const((1, d)), const((1, d))],
        out_specs=pl.BlockSpec((tm, d), lambda i: (i, 0)),
        out_shape=jax.ShapeDtypeStruct((n, d), F32),
        scratch_shapes=[pltpu.VMEM((2, TOP_K, tm) + ROW_TILE, F32), pltpu.SemaphoreType.DMA((2,))],
        compiler_params=_params("arbitrary"),
        name="combine",
    )(dest, dest, h, gate_t, ys, wsg, wsu, wsd, ln_g.reshape(1, d), ln_b.reshape(1, d))


def _key_major(cache):
    b, h, s, hd = cache.shape
    return cache.transpose(0, 2, 1, 3).reshape(b, s, h * hd).astype(BF16)


def kernel(x_prompt, x_sample, cache_k, cache_v, state_pool, ln0_g, ln0_b, w_in, w_sb_up, w_pool_grp, pool_scale,
           w_pool_up, w_o, ln1_g, ln1_b, w_router, router_bias, w_exp_gate, w_exp_up, w_exp_down, w_sh_gate,
           w_sh_up, w_sh_down, ln2_g, ln2_b):
    assert w_in.shape[0] == DEPTH
    b_p, t_p, d = x_prompt.shape
    b_s, t_s, _ = x_sample.shape
    past = cache_k.shape[3]
    n_p, n_s = b_p * t_p, b_s * t_s
    layer = 0
    w_in_b = w_in[layer].astype(BF16)
    mix_w = (w_sb_up[layer].astype(BF16), w_pool_grp[layer].astype(BF16), pool_scale[layer],
             w_pool_up[layer].astype(BF16), w_o[layer].astype(BF16), ln1_g[layer], ln1_b[layer])

    qb, kb, vb, k_p, v_p, u_p, sg_p = _proj(x_prompt, ln0_g, ln0_b, w_in_b)
    osb_p = _attn(qb, kb, vb, q_pos0=0)
    hist_p = jnp.zeros((b_p, HALO, POOL_WIDTH), F32)
    h_p = _merge(x_prompt, osb_p, u_p, hist_p, sg_p, ln0_g, ln0_b, *mix_w, hist_valid=0)

    qs, ks, vs, k_s, v_s, u_s, sg_s = _proj(x_sample, ln0_g, ln0_b, w_in_b)
    key_rows = -(-(past + t_s) // KEY_TILE) * KEY_TILE
    pad = ((0, 0), (0, key_rows - past - t_s), (0, 0))
    k_all = jnp.pad(jnp.concatenate([_key_major(cache_k[layer]), ks], axis=1), pad)
    v_all = jnp.pad(jnp.concatenate([_key_major(cache_v[layer]), vs], axis=1), pad)
    osb_s = _attn(qs, k_all, v_all, q_pos0=past)
    hist_s = jnp.concatenate([jnp.zeros((b_s, 1, POOL_WIDTH), F32), state_pool[layer]], axis=1)
    h_s = _merge(x_sample, osb_s, u_s, hist_s, sg_s, ln0_g, ln0_b, *mix_w, hist_valid=POOL_STATE)

    wr_t = w_router[layer].T.astype(BF16)
    bias = router_bias[layer].astype(F32).reshape(N_EXPERTS, 1)
    idx_p, gate_p, rank_p, count = _route(h_p, wr_t, bias, jnp.zeros((N_EXPERTS, 1), F32))
    idx_s, gate_s, rank_s, count = _route(h_s, wr_t, bias, count)

    blk = EXPERT_BLOCK
    n_blocks = ((n_p + n_s) * TOP_K + N_EXPERTS * (blk - 1) + blk - 1) // blk
    cap = n_blocks * blk
    assert cap < 2 ** 24
    nblk = (count[:, 0].astype(I32) + blk - 1) // blk
    first = jnp.cumsum(nblk) - nblk
    used = jnp.sum(nblk).reshape(1)
    start = (first * blk).astype(F32).reshape(N_EXPERTS, 1)
    dest_p = _slots(idx_p, rank_p, start).T.reshape(-1)
    dest_s = _slots(idx_s, rank_s, start).T.reshape(-1)

    xs = _dispatch(dest_p, h_p, None, cap)
    xs = _dispatch(dest_s, h_s, xs, cap)
    ys = _experts(first, nblk, used, xs, w_exp_gate[layer], w_exp_up[layer], w_exp_down[layer])

    shared_w = (w_sh_gate[layer].astype(BF16), w_sh_up[layer].astype(BF16), w_sh_down[layer].astype(BF16),
                ln2_g[layer], ln2_b[layer])
    y_p = _combine(dest_p, h_p, gate_p.T, ys, *shared_w)
    y_s = _combine(dest_s, h_s, gate_s.T, ys, *shared_w)

    new_pool_p = u_p[:, t_p - POOL_STATE:][None]
    new_pool_s = jnp.concatenate([state_pool[layer], u_s], axis=1)[:, -POOL_STATE:][None]
    return (y_p.reshape(b_p, t_p, d), y_s.reshape(b_s, t_s, d), k_p, v_p, new_pool_p, k_s, v_s, new_pool_s)
```

```python
import functools

import jax
import jax.numpy as jnp
from jax import lax
from jax.experimental import pallas as pl
from jax.experimental.pallas import tpu as pltpu

F32 = jnp.float32
BF16 = jnp.bfloat16
I32 = jnp.int32

D_MODEL = 1024
SB_HEADS = 8
SB_HEAD_DIM = 64
SB_WIDTH = SB_HEADS * SB_HEAD_DIM
SB_SCALE = SB_HEAD_DIM ** -0.5
POOL_WINDOWS = (2, 4, 8, 16)
POOL_WIDTH = D_MODEL // 2
POOL_GROUP_WIDTH = POOL_WIDTH // len(POOL_WINDOWS)
POOL_STATE = max(POOL_WINDOWS) - 1
HALO = POOL_STATE + 1
N_EXPERTS = 256
TOP_K = 8
N_EXPERT_GROUPS = 8
TOPK_GROUPS = 4
GROUP_SIZE = N_EXPERTS // N_EXPERT_GROUPS
EXPERT_DIM = D_MODEL // 4
ROUTED_SCALE = 2.5
LN_EPS = 1e-5
DEPTH = 1
DEEPNORM_ALPHA = (2 * DEPTH) ** 0.25

SUBLANES, LANES = 8, 128
ROW_TILE = (SUBLANES, LANES)
assert SUBLANES * LANES == D_MODEL
EXPERT_X_AHEAD = 5
EXPERT_X_SLOTS = EXPERT_X_AHEAD + 2
EXPERT_Y_SLOTS = 4
KEY_TILE = 128
MAX_QUERY_TILE = 128
ATTN_STREAMS = 2
EXPERT_BLOCK = 256
TOKEN_TILE = 512
GATHER_TILE = 256
EXP_UNDERFLOW = 104.0
VMEM_LIMIT = 56 * 1024 * 1024

_NT = (((1,), (1,)), ((), ()))


def _params(*semantics):
    return pltpu.CompilerParams(dimension_semantics=semantics, vmem_limit_bytes=VMEM_LIMIT)


def _layer_norm(x, g, b):
    mu = jnp.mean(x, axis=-1, keepdims=True)
    xc = x - mu
    var = jnp.mean(xc * xc, axis=-1, keepdims=True)
    return xc * lax.rsqrt(var + LN_EPS) * g + b


def _sigmoid(x):
    return 1.0 / (1.0 + jnp.exp(-x))


def _dot(a, b):
    return jnp.dot(a, b, preferred_element_type=F32)


def _to_tiles(x):
    return x.reshape((x.shape[0],) + ROW_TILE)


def _from_tiles(x):
    return x.reshape(x.shape[0], D_MODEL)


def _proj_kernel(x_ref, g_ref, b_ref, w_ref, qb_ref, kb_ref, vb_ref, k_ref, v_ref, u_ref, sg_ref):
    h = _layer_norm(x_ref[0], g_ref[...], b_ref[...]).astype(BF16)

    def proj(lo, hi):
        return _dot(h, w_ref[:, lo:hi])

    qb_ref[0] = (proj(0, SB_WIDTH) * SB_SCALE).astype(BF16)
    zk = proj(SB_WIDTH, 2 * SB_WIDTH)
    kb_ref[0] = zk.astype(BF16)
    zv = proj(2 * SB_WIDTH, 3 * SB_WIDTH)
    vb_ref[0] = zv.astype(BF16)
    for hd in range(SB_HEADS):
        sl = slice(hd * SB_HEAD_DIM, (hd + 1) * SB_HEAD_DIM)
        k_ref[0, 0, hd] = zk[:, sl]
        v_ref[0, 0, hd] = zv[:, sl]
    u0 = 3 * SB_WIDTH
    u_ref[0] = proj(u0, u0 + POOL_WIDTH)
    g0 = u0 + POOL_WIDTH
    for c in range(2 * D_MODEL // SB_WIDTH):
        sl = slice(c * SB_WIDTH, (c + 1) * SB_WIDTH)
        sg_ref[0, :, sl] = _sigmoid(proj(g0 + c * SB_WIDTH, g0 + (c + 1) * SB_WIDTH)).astype(BF16)


def _proj(x, ln_g, ln_b, w_in_b):
    b, t, d = x.shape
    tm = min(TOKEN_TILE, t)
    in_width = w_in_b.shape[1]
    row = lambda width: pl.BlockSpec((1, tm, width), lambda i, j: (i, j, 0))
    heads = pl.BlockSpec((1, 1, SB_HEADS, tm, SB_HEAD_DIM), lambda i, j: (0, i, 0, j, 0))
    const = lambda shape: pl.BlockSpec(shape, lambda i, j: (0,) * len(shape))
    head_shape = jax.ShapeDtypeStruct((1, b, SB_HEADS, t, SB_HEAD_DIM), F32)
    return pl.pallas_call(
        _proj_kernel,
        grid=(b, t // tm),
        in_specs=[row(d), const((1, d)), const((1, d)), const((d, in_width))],
        out_specs=[row(SB_WIDTH), row(SB_WIDTH), row(SB_WIDTH), heads, heads, row(POOL_WIDTH), row(2 * d)],
        out_shape=[
            jax.ShapeDtypeStruct((b, t, SB_WIDTH), BF16),
            jax.ShapeDtypeStruct((b, t, SB_WIDTH), BF16),
            jax.ShapeDtypeStruct((b, t, SB_WIDTH), BF16),
            head_shape,
            head_shape,
            jax.ShapeDtypeStruct((b, t, POOL_WIDTH), F32),
            jax.ShapeDtypeStruct((b, t, 2 * d), BF16),
        ],
        compiler_params=_params("parallel", "parallel"),
        name="proj",
    )(x, ln_g.reshape(1, d), ln_b.reshape(1, d), w_in_b)


def _attn_kernel(q_ref, k_ref, v_ref, tri_ref, o_ref, acc_ref, carry_ref, qm_ref, *, tq, q_pos0, streams):
    tk = KEY_TILE
    pairs = SB_HEADS // 2
    pair_lanes = [slice(p * 2 * SB_HEAD_DIM, (p + 1) * 2 * SB_HEAD_DIM) for p in range(pairs)]
    q_first = q_pos0 + pl.program_id(1) * tq
    low_half = lax.broadcasted_iota(I32, (1, 2 * SB_HEAD_DIM), 1) < SB_HEAD_DIM

    def split_heads(x2):
        zero = jnp.zeros_like(x2)
        return jnp.concatenate([jnp.where(low_half, x2, zero), jnp.where(low_half, zero, x2)], axis=0)

    acc_ref[...] = jnp.zeros_like(acc_ref)
    carry_ref[...] = jnp.zeros_like(carry_ref)
    for b in range(streams):
        for p in range(pairs):
            qm_ref[b * pairs + p] = split_heads(-q_ref[b, :, pair_lanes[p]])

    def step(j, masked):
        k0 = pl.multiple_of(j * tk, tk)
        k_tiles = [k_ref[b, pl.ds(k0, tk), :] for b in range(streams)]
        neg = jnp.concatenate(
            [lax.dot_general(qm_ref[b * pairs + p], k_tiles[b][:, pair_lanes[p]], _NT, preferred_element_type=F32)
             for b in range(streams) for p in range(pairs)], axis=0).reshape(streams * SB_HEADS, tq, tk)
        soft = jnp.log(1.0 + jnp.exp(-jnp.abs(neg)))
        log_rest = jnp.minimum(neg, 0.0) - soft
        log_beta = log_rest - neg
        if masked:
            mask = ((k0 + lax.broadcasted_iota(I32, (tq, tk), 1))
                    < (q_first + lax.broadcasted_iota(I32, (tq, tk), 0)))[None]
            log_rest = jnp.where(mask, log_rest, 0.0)
        hi = log_rest.astype(BF16)
        lo = (log_rest - hi.astype(F32)).astype(BF16)
        sums = _dot(jnp.concatenate([hi, lo], axis=-1).reshape(streams * SB_HEADS * tq, 2 * tk), tri_ref[...])
        sums = sums.reshape(streams * SB_HEADS, tq, 2 * tk)
        carry = carry_ref[...]
        a = jnp.exp(log_beta + sums[:, :, :tk] + carry)
        if masked:
            a = jnp.where(mask, a, 0.0)
        a = a.astype(BF16)
        carry = carry + sums[:, :, tk:]
        carry_ref[...] = carry
        for b in range(streams):
            v_tile = v_ref[b, pl.ds(k0, tk), :]
            for p in range(pairs):
                head = b * SB_HEADS + 2 * p
                a2 = jnp.concatenate([a[head], a[head + 1]], axis=1)
                acc_ref[b, :, pair_lanes[p]] += _dot(a2, split_heads(v_tile[:, pair_lanes[p]]))
        return (jnp.max(carry) > -EXP_UNDERFLOW).astype(I32)

    j_first = (q_first + tq - 1) // tk
    go = step(j_first, True)
    lax.while_loop(lambda st: jnp.logical_and(st[0] >= 0, st[1] > 0),
                   lambda st: (st[0] - 1, step(st[0], False)), (j_first - 1, go))
    o_ref[...] = acc_ref[...].astype(BF16)


def _attn(qb, kb, vb, *, q_pos0):
    b, t, w = qb.shape
    s_k = kb.shape[1]
    tq = min(MAX_QUERY_TILE, t)
    tk = KEY_TILE
    g = ATTN_STREAMS if b % ATTN_STREAMS == 0 else 1
    assert t % tq == 0 and s_k % tk == 0 and s_k >= q_pos0 + t
    assert tk % tq == 0 and q_pos0 % tq == 0
    r = lax.broadcasted_iota(I32, (2 * tk, 2 * tk), 0) % tk
    c = lax.broadcasted_iota(I32, (2 * tk, 2 * tk), 1)
    tri = jnp.logical_or(r > c, c >= tk).astype(BF16)
    return pl.pallas_call(
        functools.partial(_attn_kernel, tq=tq, q_pos0=q_pos0, streams=g),
        grid=(b // g, t // tq),
        in_specs=[
            pl.BlockSpec((g, tq, w), lambda i, j: (i, j, 0)),
            pl.BlockSpec((g, s_k, w), lambda i, j: (i, 0, 0)),
            pl.BlockSpec((g, s_k, w), lambda i, j: (i, 0, 0)),
            pl.BlockSpec((2 * tk, 2 * tk), lambda i, j: (0, 0)),
        ],
        out_specs=pl.BlockSpec((g, tq, w), lambda i, j: (i, j, 0)),
        out_shape=jax.ShapeDtypeStruct((b, t, w), BF16),
        scratch_shapes=[pltpu.VMEM((g, tq, w), F32), pltpu.VMEM((g * SB_HEADS, tq, tk), F32),
                        pltpu.VMEM((g * SB_HEADS // 2, 2 * tq, 2 * SB_HEAD_DIM), BF16)],
        compiler_params=_params("parallel", "parallel"),
        name="attn",
    )(qb, kb, vb, tri)


def _merge_kernel(x_ref, osb_ref, u_ref, uprev_ref, hist_ref, sg_ref, g0_ref, b0_ref, wsb_ref, wgrp_ref,
                  pscale_ref, wpool_ref, wo_ref, g1_ref, b1_ref, h_ref, ext_ref, *, tm, hist_valid):
    t = pl.program_id(1)
    h0 = _layer_norm(x_ref[0], g0_ref[...], b0_ref[...])
    u = u_ref[0]
    ext_ref[:HALO, :] = jnp.where(t == 0, hist_ref[0], uprev_ref[0])
    ext_ref[HALO:, :] = u
    rows_seen = t * tm + lax.broadcasted_iota(I32, (tm, 1), 0) + (1 + hist_valid)
    pool_out = []
    for g, window in enumerate(POOL_WINDOWS):
        lanes = slice(g * POOL_GROUP_WIDTH, (g + 1) * POOL_GROUP_WIDTH)
        total = u[:, lanes]
        for back in range(1, window):
            total = total + ext_ref[HALO - back:HALO - back + tm, lanes]
        count = jnp.minimum(rows_seen, window).astype(F32)
        pooled = total / count - u[:, lanes]
        pool_out.append(_dot(pooled.astype(BF16), wgrp_ref[g]))
    o_pool = jnp.concatenate(pool_out, axis=1) * pscale_ref[...]
    a = _dot(osb_ref[0], wsb_ref[...])
    p = _dot(o_pool.astype(BF16), wpool_ref[...])
    sg = sg_ref[0].astype(F32)
    m = sg[:, :D_MODEL] * a + sg[:, D_MODEL:] * p
    mix = _dot(m.astype(BF16), wo_ref[...])
    h1 = _layer_norm(DEEPNORM_ALPHA * h0 + mix, g1_ref[...], b1_ref[...])
    h_ref[...] = _to_tiles(h1)


def _merge(x, osb, u, hist, sg, ln0_g, ln0_b, wsb, wgrp, pscale, wpool, wo, ln1_g, ln1_b, *, hist_valid):
    b, t, d = x.shape
    tm = min(TOKEN_TILE, t)
    nt = t // tm
    row = lambda width: pl.BlockSpec((1, tm, width), lambda i, j: (i, j, 0))
    const = lambda shape: pl.BlockSpec(shape, lambda i, j: (0,) * len(shape))
    prev = pl.BlockSpec((1, HALO, POOL_WIDTH), lambda i, j: (i, jnp.maximum(j * (tm // HALO) - 1, 0), 0))
    return pl.pallas_call(
        functools.partial(_merge_kernel, tm=tm, hist_valid=hist_valid),
        grid=(b, nt),
        in_specs=[
            row(d), row(SB_WIDTH), row(POOL_WIDTH), prev,
            pl.BlockSpec((1, HALO, POOL_WIDTH), lambda i, j: (i, 0, 0)),
            row(2 * d), const((1, d)), const((1, d)), const((SB_WIDTH, d)),
            const((len(POOL_WINDOWS), POOL_GROUP_WIDTH, POOL_GROUP_WIDTH)), const((1, POOL_WIDTH)),
            const((POOL_WIDTH, d)), const((d, d)), const((1, d)), const((1, d)),
        ],
        out_specs=pl.BlockSpec((tm,) + ROW_TILE, lambda i, j: (i * nt + j, 0, 0)),
        out_shape=jax.ShapeDtypeStruct((b * t,) + ROW_TILE, F32),
        scratch_shapes=[pltpu.VMEM((HALO + tm, POOL_WIDTH), F32)],
        compiler_params=_params("parallel", "parallel"),
        name="merge",
    )(x, osb, u, u, hist, sg, ln0_g.reshape(1, d), ln0_b.reshape(1, d), wsb, wgrp, pscale.reshape(1, -1),
      wpool, wo, ln1_g.reshape(1, d), ln1_b.reshape(1, d))


def _first_max(vals, index, n):
    top = jnp.max(vals, axis=0, keepdims=True)
    first = jnp.min(jnp.where(vals == top, index, n), axis=0, keepdims=True)
    return top, first


def _route_kernel(h_ref, wr_ref, bias_ref, tri_ref, cin_ref, idx_ref, gate_ref, rank_ref, cout_ref, *, tm):
    @pl.when(pl.program_id(0) == 0)
    def _():
        cout_ref[...] = cin_ref[...]

    h = _from_tiles(h_ref[...]).astype(BF16)
    scores = _sigmoid(lax.dot_general(wr_ref[...], h, _NT, preferred_element_type=F32))
    biased = scores + bias_ref[...]
    neg = jnp.float32(-jnp.inf)

    in_group = lax.broadcasted_iota(I32, (GROUP_SIZE, tm), 0)
    group_scores = []
    for g in range(N_EXPERT_GROUPS):
        blk = biased[g * GROUP_SIZE:(g + 1) * GROUP_SIZE, :]
        top, first = _first_max(blk, in_group, GROUP_SIZE)
        second = jnp.max(jnp.where(in_group == first, neg, blk), axis=0, keepdims=True)
        group_scores.append(top + second)
    remaining = jnp.concatenate(group_scores, axis=0)
    group_id = lax.broadcasted_iota(I32, (N_EXPERT_GROUPS, tm), 0)
    kept = jnp.zeros((N_EXPERT_GROUPS, tm), F32)
    for _ in range(TOPK_GROUPS):
        _, first = _first_max(remaining, group_id, N_EXPERT_GROUPS)
        hit = group_id == first
        kept = jnp.where(hit, 1.0, kept)
        remaining = jnp.where(hit, neg, remaining)
    remaining = jnp.concatenate(
        [jnp.where(kept[g:g + 1, :] > 0.5, biased[g * GROUP_SIZE:(g + 1) * GROUP_SIZE, :], neg)
         for g in range(N_EXPERT_GROUPS)], axis=0)

    expert_id = lax.broadcasted_iota(I32, (N_EXPERTS, tm), 0)
    chosen, picks, picked_scores = jnp.zeros((N_EXPERTS, tm), F32), [], []
    for _ in range(TOP_K):
        _, first = _first_max(remaining, expert_id, N_EXPERTS)
        hit = expert_id == first
        picks.append((first, hit))
        picked_scores.append(jnp.sum(jnp.where(hit, scores, 0.0), axis=0, keepdims=True))
        chosen = jnp.where(hit, 1.0, chosen)
        remaining = jnp.where(hit, neg, remaining)
    denom = picked_scores[0]
    for sc in picked_scores[1:]:
        denom = denom + sc
    gate_ref[...] = jnp.concatenate([sc / denom * ROUTED_SCALE for sc in picked_scores], axis=0)
    idx_ref[...] = jnp.concatenate([first for first, _ in picks], axis=0)

    slot = cout_ref[...] + _dot(chosen.astype(BF16), tri_ref[...])
    rank_ref[...] = jnp.concatenate(
        [jnp.sum(jnp.where(hit, slot, 0.0), axis=0, keepdims=True) for _, hit in picks], axis=0).astype(I32)
    cout_ref[...] += jnp.sum(chosen, axis=1, keepdims=True)


def _route(h, wr_t, bias, count_in):
    n = h.shape[0]
    tm = min(TOKEN_TILE, n)
    assert n % tm == 0
    earlier = (lax.broadcasted_iota(I32, (tm, tm), 0) < lax.broadcasted_iota(I32, (tm, tm), 1)).astype(BF16)
    const = lambda shape: pl.BlockSpec(shape, lambda i: (0,) * len(shape))
    picks = pl.BlockSpec((TOP_K, tm), lambda i: (0, i))
    return pl.pallas_call(
        functools.partial(_route_kernel, tm=tm),
        grid=(n // tm,),
        in_specs=[pl.BlockSpec((tm,) + ROW_TILE, lambda i: (i, 0, 0)), const((N_EXPERTS, D_MODEL)),
                  const((N_EXPERTS, 1)), const((tm, tm)), const((N_EXPERTS, 1))],
        out_specs=[picks, picks, picks, const((N_EXPERTS, 1))],
        out_shape=[jax.ShapeDtypeStruct((TOP_K, n), I32), jax.ShapeDtypeStruct((TOP_K, n), F32),
                   jax.ShapeDtypeStruct((TOP_K, n), I32), jax.ShapeDtypeStruct((N_EXPERTS, 1), F32)],
        compiler_params=_params("arbitrary"),
        name="route",
    )(h, wr_t, bias, earlier, count_in)


def _slot_kernel(idx_ref, rank_ref, start_ref, dest_ref, *, tm):
    expert_id = lax.broadcasted_iota(I32, (N_EXPERTS, tm), 0)
    start = start_ref[...]
    rows = [jnp.sum(jnp.where(expert_id == idx_ref[j:j + 1, :], start, 0.0), axis=0, keepdims=True)
            for j in range(TOP_K)]
    dest_ref[...] = jnp.concatenate(rows, axis=0).astype(I32) + rank_ref[...]


def _slots(idx, rank, start):
    n = idx.shape[1]
    tm = min(TOKEN_TILE, n)
    assert n % tm == 0
    picks = pl.BlockSpec((TOP_K, tm), lambda i: (0, i))
    return pl.pallas_call(
        functools.partial(_slot_kernel, tm=tm),
        grid=(n // tm,),
        in_specs=[picks, picks, pl.BlockSpec((N_EXPERTS, 1), lambda i: (0, 0))],
        out_specs=picks,
        out_shape=jax.ShapeDtypeStruct((TOP_K, n), I32),
        compiler_params=_params("parallel"),
        name="slots",
    )(idx, rank, start)


def _start_row_copies(tm, copy_of):
    def group(g, carry):
        tok0 = pl.multiple_of(g * SUBLANES, SUBLANES)
        for r in range(SUBLANES):
            for j in range(TOP_K):
                copy_of(tok0 + r, j, tok0 * TOP_K + (r * TOP_K + j)).start(priority=(r + j) % 2)
        return carry

    lax.fori_loop(0, tm // SUBLANES, group, 0)


def _dispatch_kernel(dest_ref, h_ref, *rest, tm):
    xs_ref, sem = rest[-2], rest[-1]
    _start_row_copies(
        tm, lambda tok, j, pair: pltpu.make_async_copy(h_ref.at[tok], xs_ref.at[dest_ref[pair]], sem))
    for _ in range(TOP_K):
        pltpu.make_async_copy(h_ref, xs_ref.at[pl.ds(0, tm)], sem).wait()


def _dispatch(dest, h, xs, cap):
    n = h.shape[0]
    tm = min(GATHER_TILE, n)
    assert n % tm == 0
    in_specs = [pl.BlockSpec((tm * TOP_K,), lambda i: (i,), memory_space=pltpu.SMEM),
                pl.BlockSpec((tm,) + ROW_TILE, lambda i: (i, 0, 0))]
    args = [dest, h]
    aliases = {}
    if xs is not None:
        in_specs.append(pl.BlockSpec(memory_space=pl.ANY))
        args.append(xs)
        aliases = {2: 0}
    return pl.pallas_call(
        functools.partial(_dispatch_kernel, tm=tm),
        grid=(n // tm,),
        in_specs=in_specs,
        out_specs=pl.BlockSpec(memory_space=pl.ANY),
        out_shape=jax.ShapeDtypeStruct((cap,) + ROW_TILE, F32),
        scratch_shapes=[pltpu.SemaphoreType.DMA],
        input_output_aliases=aliases,
        compiler_params=pltpu.CompilerParams(dimension_semantics=("arbitrary",), vmem_limit_bytes=VMEM_LIMIT,
                                             has_side_effects=True),
        name="dispatch",
    )(*args)


def _expert_kernel(first_ref, nblk_ref, used_ref, wg_ref, wu_ref, wd_ref, xs_ref, ys_ref,
                   wg_b, wu_b, wd_b, x_buf, y_buf, x_sem, y_sem):
    e = pl.program_id(0)
    blk, nx, ny, ahead = EXPERT_BLOCK, EXPERT_X_SLOTS, EXPERT_Y_SLOTS, EXPERT_X_AHEAD
    first, n, used = first_ref[e], nblk_ref[e], used_ref[0]

    def x_copy(g):
        slot = g % nx
        return pltpu.make_async_copy(xs_ref.at[pl.ds(pl.multiple_of(g * blk, blk), blk)], x_buf.at[slot],
                                     x_sem.at[slot])

    def y_copy(g):
        slot = g % ny
        return pltpu.make_async_copy(y_buf.at[slot], ys_ref.at[pl.ds(pl.multiple_of(g * blk, blk), blk)],
                                     y_sem.at[slot])

    @pl.when(e == 0)
    def _():
        for g in range(ahead):
            @pl.when(g < used)
            def _():
                x_copy(g).start()

    def compute(blocks):
        for g in blocks:
            @pl.when(g + ahead < used)
            def _():
                x_copy(g + ahead).start()
        for g in blocks:
            x_copy(g).wait()
        x = jnp.concatenate([_from_tiles(x_buf[g % nx]) for g in blocks], axis=0).astype(BF16)
        gate = _dot(x, wg_b[...])
        hid = gate * _sigmoid(gate) * _dot(x, wu_b[...])
        y = _dot(hid.astype(BF16), wd_b[...])
        for k, g in enumerate(blocks):
            @pl.when(g >= ny)
            def _():
                y_copy(g - ny).wait()

            y_buf[g % ny] = _to_tiles(y[k * blk:(k + 1) * blk])
            y_copy(g).start()

    @pl.when(n > 0)
    def _():
        wg_b[...] = wg_ref[0].astype(BF16)
        wu_b[...] = wu_ref[0].astype(BF16)
        wd_b[...] = wd_ref[0].astype(BF16)

        def pair(p, carry):
            compute([first + 2 * p, first + 2 * p + 1])
            return carry

        lax.fori_loop(0, n // 2, pair, 0)

        @pl.when(n % 2 == 1)
        def _():
            compute([first + n - 1])

    @pl.when(e == pl.num_programs(0) - 1)
    def _():
        for back in range(ny, 0, -1):
            @pl.when(used >= back)
            def _():
                y_copy(used - back).wait()


def _experts(first, nblk, used, xs, w_gate, w_up, w_down):
    cap = xs.shape[0]
    blk = EXPERT_BLOCK
    gate_up = pl.BlockSpec((1, D_MODEL, EXPERT_DIM), lambda e, *_: (e, 0, 0))
    return pl.pallas_call(
        _expert_kernel,
        grid_spec=pltpu.PrefetchScalarGridSpec(
            num_scalar_prefetch=3,
            grid=(N_EXPERTS,),
            in_specs=[gate_up, gate_up, pl.BlockSpec((1, EXPERT_DIM, D_MODEL), lambda e, *_: (e, 0, 0)),
                      pl.BlockSpec(memory_space=pl.ANY)],
            out_specs=pl.BlockSpec(memory_space=pl.ANY),
            scratch_shapes=[pltpu.VMEM((D_MODEL, EXPERT_DIM), BF16), pltpu.VMEM((D_MODEL, EXPERT_DIM), BF16),
                            pltpu.VMEM((EXPERT_DIM, D_MODEL), BF16),
                            pltpu.VMEM((EXPERT_X_SLOTS, blk) + ROW_TILE, F32),
                            pltpu.VMEM((EXPERT_Y_SLOTS, blk) + ROW_TILE, F32),
                            pltpu.SemaphoreType.DMA((EXPERT_X_SLOTS,)), pltpu.SemaphoreType.DMA((EXPERT_Y_SLOTS,))],
        ),
        out_shape=jax.ShapeDtypeStruct((cap,) + ROW_TILE, F32),
        compiler_params=_params("arbitrary"),
        name="experts",
    )(first, nblk, used, w_gate, w_up, w_down, xs)


def _combine_kernel(dest0_ref, dest1_ref, h_ref, gate_ref, ys_ref, wsg_ref, wsu_ref, wsd_ref, g_ref, b_ref, o_ref,
                    rows_ref, sem, *, tm):
    i = pl.program_id(0)
    slot = i % 2

    def gather(dest_ref, into):
        _start_row_copies(tm, lambda tok, j, pair: pltpu.make_async_copy(
            ys_ref.at[dest_ref[pair]], rows_ref.at[into, j, tok], sem.at[into]))

    @pl.when(i == 0)
    def _():
        gather(dest0_ref, 0)

    @pl.when(i + 1 < pl.num_programs(0))
    def _():
        gather(dest1_ref, 1 - slot)

    h = _from_tiles(h_ref[...])
    hb = h.astype(BF16)
    gate = _dot(hb, wsg_ref[...])
    hid = gate * _sigmoid(gate) * _dot(hb, wsu_ref[...])
    ffn = _dot(hid.astype(BF16), wsd_ref[...])
    for j in range(TOP_K):
        pltpu.make_async_copy(ys_ref.at[pl.ds(0, tm)], rows_ref.at[slot, j], sem.at[slot]).wait()
    weights = gate_ref[...]
    for j in range(TOP_K):
        ffn = ffn + weights[:, j:j + 1] * _from_tiles(rows_ref[slot, j])
    o_ref[...] = _layer_norm(DEEPNORM_ALPHA * h + ffn, g_ref[...], b_ref[...])


def _combine(dest, h, gate_t, ys, wsg, wsu, wsd, ln_g, ln_b):
    n, d = h.shape[0], D_MODEL
    tm = min(GATHER_TILE, n)
    assert n % tm == 0
    const = lambda shape: pl.BlockSpec(shape, lambda i: (0,) * len(shape))
    steps = n // tm
    return pl.pallas_call(
        functools.partial(_combine_kernel, tm=tm),
        grid=(steps,),
        in_specs=[pl.BlockSpec((tm * TOP_K,), lambda i: (0,), memory_space=pltpu.SMEM),
                  pl.BlockSpec((tm * TOP_K,), lambda i: (jnp.minimum(i + 1, steps - 1),), memory_space=pltpu.SMEM),
                  pl.BlockSpec((tm,) + ROW_TILE, lambda i: (i, 0, 0)),
                  pl.BlockSpec((tm, TOP_K), lambda i: (i, 0)),
                  pl.BlockSpec(memory_space=pl.ANY),
                  const((d, EXPERT_DIM)), const((d, EXPERT_DIM)), const((EXPERT_DIM, d)),
                  const((1, d)), const((1, d))],
        out_specs=pl.BlockSpec((tm, d), lambda i: (i, 0)),
        out_shape=jax.ShapeDtypeStruct((n, d), F32),
        scratch_shapes=[pltpu.VMEM((2, TOP_K, tm) + ROW_TILE, F32), pltpu.SemaphoreType.DMA((2,))],
        compiler_params=_params("arbitrary"),
        name="combine",
    )(dest, dest, h, gate_t, ys, wsg, wsu, wsd, ln_g.reshape(1, d), ln_b.reshape(1, d))


def _key_major(cache):
    b, h, s, hd = cache.shape
    return cache.transpose(0, 2, 1, 3).reshape(b, s, h * hd).astype(BF16)


def kernel(x_prompt, x_sample, cache_k, cache_v, state_pool, ln0_g, ln0_b, w_in, w_sb_up, w_pool_grp, pool_scale,
           w_pool_up, w_o, ln1_g, ln1_b, w_router, router_bias, w_exp_gate, w_exp_up, w_exp_down, w_sh_gate,
           w_sh_up, w_sh_down, ln2_g, ln2_b):
    assert w_in.shape[0] == DEPTH
    b_p, t_p, d = x_prompt.shape
    b_s, t_s, _ = x_sample.shape
    past = cache_k.shape[3]
    n_p, n_s = b_p * t_p, b_s * t_s
    layer = 0
    w_in_b = w_in[layer].astype(BF16)
    mix_w = (w_sb_up[layer].astype(BF16), w_pool_grp[layer].astype(BF16), pool_scale[layer],
             w_pool_up[layer].astype(BF16), w_o[layer].astype(BF16), ln1_g[layer], ln1_b[layer])

    qb, kb, vb, k_p, v_p, u_p, sg_p = _proj(x_prompt, ln0_g, ln0_b, w_in_b)
    osb_p = _attn(qb, kb, vb, q_pos0=0)
    hist_p = jnp.zeros((b_p, HALO, POOL_WIDTH), F32)
    h_p = _merge(x_prompt, osb_p, u_p, hist_p, sg_p, ln0_g, ln0_b, *mix_w, hist_valid=0)

    qs, ks, vs, k_s, v_s, u_s, sg_s = _proj(x_sample, ln0_g, ln0_b, w_in_b)
    key_rows = -(-(past + t_s) // KEY_TILE) * KEY_TILE
    pad = ((0, 0), (0, key_rows - past - t_s), (0, 0))
    k_all = jnp.pad(jnp.concatenate([_key_major(cache_k[layer]), ks], axis=1), pad)
    v_all = jnp.pad(jnp.concatenate([_key_major(cache_v[layer]), vs], axis=1), pad)
    osb_s = _attn(qs, k_all, v_all, q_pos0=past)
    hist_s = jnp.concatenate([jnp.zeros((b_s, 1, POOL_WIDTH), F32), state_pool[layer]], axis=1)
    h_s = _merge(x_sample, osb_s, u_s, hist_s, sg_s, ln0_g, ln0_b, *mix_w, hist_valid=POOL_STATE)

    wr_t = w_router[layer].T.astype(BF16)
    bias = router_bias[layer].astype(F32).reshape(N_EXPERTS, 1)
    idx_p, gate_p, rank_p, count = _route(h_p, wr_t, bias, jnp.zeros((N_EXPERTS, 1), F32))
    idx_s, gate_s, rank_s, count = _route(h_s, wr_t, bias, count)

    blk = EXPERT_BLOCK
    n_blocks = ((n_p + n_s) * TOP_K + N_EXPERTS * (blk - 1) + blk - 1) // blk
    cap = n_blocks * blk
    assert cap < 2 ** 24
    nblk = (count[:, 0].astype(I32) + blk - 1) // blk
    first = jnp.cumsum(nblk) - nblk
    used = jnp.sum(nblk).reshape(1)
    start = (first * blk).astype(F32).reshape(N_EXPERTS, 1)
    dest_p = _slots(idx_p, rank_p, start).T.reshape(-1)
    dest_s = _slots(idx_s, rank_s, start).T.reshape(-1)

    xs = _dispatch(dest_p, h_p, None, cap)
    xs = _dispatch(dest_s, h_s, xs, cap)
    ys = _experts(first, nblk, used, xs, w_exp_gate[layer], w_exp_up[layer], w_exp_down[layer])

    shared_w = (w_sh_gate[layer].astype(BF16), w_sh_up[layer].astype(BF16), w_sh_down[layer].astype(BF16),
                ln2_g[layer], ln2_b[layer])
    y_p = _combine(dest_p, h_p, gate_p.T, ys, *shared_w)
    y_s = _combine(dest_s, h_s, gate_s.T, ys, *shared_w)

    new_pool_p = u_p[:, t_p - POOL_STATE:][None]
    new_pool_s = jnp.concatenate([state_pool[layer], u_s], axis=1)[:, -POOL_STATE:][None]
    return (y_p.reshape(b_p, t_p, d), y_s.reshape(b_s, t_s, d), k_p, v_p, new_pool_p, k_s, v_s, new_pool_s)
```

```python
import functools

import jax
import jax.numpy as jnp
from jax import lax
from jax.experimental import pallas as pl
from jax.experimental.pallas import tpu as pltpu

F32 = jnp.float32
BF16 = jnp.bfloat16
I32 = jnp.int32

D_MODEL = 1024
SB_HEADS = 8
SB_HEAD_DIM = 64
SB_WIDTH = SB_HEADS * SB_HEAD_DIM
SB_SCALE = SB_HEAD_DIM ** -0.5
POOL_WINDOWS = (2, 4, 8, 16)
POOL_WIDTH = D_MODEL // 2
POOL_GROUP_WIDTH = POOL_WIDTH // len(POOL_WINDOWS)
POOL_STATE = max(POOL_WINDOWS) - 1
HALO = POOL_STATE + 1
N_EXPERTS = 256
TOP_K = 8
N_EXPERT_GROUPS = 8
TOPK_GROUPS = 4
GROUP_SIZE = N_EXPERTS // N_EXPERT_GROUPS
EXPERT_DIM = D_MODEL // 4
ROUTED_SCALE = 2.5
LN_EPS = 1e-5
DEPTH = 1
DEEPNORM_ALPHA = (2 * DEPTH) ** 0.25

SUBLANES, LANES = 8, 128
ROW_TILE = (SUBLANES, LANES)
assert SUBLANES * LANES == D_MODEL
EXPERT_X_AHEAD = 5
EXPERT_X_SLOTS = EXPERT_X_AHEAD + 2
EXPERT_Y_SLOTS = 4
KEY_TILE = 128
MAX_QUERY_TILE = 128
ATTN_STREAMS = 4
EXPERT_BLOCK = 256
TOKEN_TILE = 512
GATHER_TILE = 256
EXP_UNDERFLOW = 104.0
VMEM_LIMIT = 56 * 1024 * 1024

_NT = (((1,), (1,)), ((), ()))


def _params(*semantics):
    return pltpu.CompilerParams(dimension_semantics=semantics, vmem_limit_bytes=VMEM_LIMIT)


def _layer_norm(x, g, b):
    mu = jnp.mean(x, axis=-1, keepdims=True)
    xc = x - mu
    var = jnp.mean(xc * xc, axis=-1, keepdims=True)
    return xc * lax.rsqrt(var + LN_EPS) * g + b


def _sigmoid(x):
    return 1.0 / (1.0 + jnp.exp(-x))


def _dot(a, b):
    return jnp.dot(a, b, preferred_element_type=F32)


def _to_tiles(x):
    return x.reshape((x.shape[0],) + ROW_TILE)


def _from_tiles(x):
    return x.reshape(x.shape[0], D_MODEL)


def _proj_kernel(x_ref, g_ref, b_ref, w_ref, qb_ref, kb_ref, vb_ref, k_ref, v_ref, u_ref, sg_ref):
    h = _layer_norm(x_ref[0], g_ref[...], b_ref[...]).astype(BF16)

    def proj(lo, hi):
        return _dot(h, w_ref[:, lo:hi])

    qb_ref[0] = (proj(0, SB_WIDTH) * SB_SCALE).astype(BF16)
    zk = proj(SB_WIDTH, 2 * SB_WIDTH)
    kb_ref[0] = zk.astype(BF16)
    zv = proj(2 * SB_WIDTH, 3 * SB_WIDTH)
    vb_ref[0] = zv.astype(BF16)
    for hd in range(SB_HEADS):
        sl = slice(hd * SB_HEAD_DIM, (hd + 1) * SB_HEAD_DIM)
        k_ref[0, 0, hd] = zk[:, sl]
        v_ref[0, 0, hd] = zv[:, sl]
    u0 = 3 * SB_WIDTH
    u_ref[0] = proj(u0, u0 + POOL_WIDTH)
    g0 = u0 + POOL_WIDTH
    for c in range(2 * D_MODEL // SB_WIDTH):
        sl = slice(c * SB_WIDTH, (c + 1) * SB_WIDTH)
        sg_ref[0, :, sl] = _sigmoid(proj(g0 + c * SB_WIDTH, g0 + (c + 1) * SB_WIDTH)).astype(BF16)


def _proj(x, ln_g, ln_b, w_in_b):
    b, t, d = x.shape
    tm = min(TOKEN_TILE, t)
    in_width = w_in_b.shape[1]
    row = lambda width: pl.BlockSpec((1, tm, width), lambda i, j: (i, j, 0))
    heads = pl.BlockSpec((1, 1, SB_HEADS, tm, SB_HEAD_DIM), lambda i, j: (0, i, 0, j, 0))
    const = lambda shape: pl.BlockSpec(shape, lambda i, j: (0,) * len(shape))
    head_shape = jax.ShapeDtypeStruct((1, b, SB_HEADS, t, SB_HEAD_DIM), F32)
    return pl.pallas_call(
        _proj_kernel,
        grid=(b, t // tm),
        in_specs=[row(d), const((1, d)), const((1, d)), const((d, in_width))],
        out_specs=[row(SB_WIDTH), row(SB_WIDTH), row(SB_WIDTH), heads, heads, row(POOL_WIDTH), row(2 * d)],
        out_shape=[
            jax.ShapeDtypeStruct((b, t, SB_WIDTH), BF16),
            jax.ShapeDtypeStruct((b, t, SB_WIDTH), BF16),
            jax.ShapeDtypeStruct((b, t, SB_WIDTH), BF16),
            head_shape,
            head_shape,
            jax.ShapeDtypeStruct((b, t, POOL_WIDTH), F32),
            jax.ShapeDtypeStruct((b, t, 2 * d), BF16),
        ],
        compiler_params=_params("parallel", "parallel"),
        name="proj",
    )(x, ln_g.reshape(1, d), ln_b.reshape(1, d), w_in_b)


def _attn_kernel(q_ref, k_ref, v_ref, tri_ref, o_ref, acc_ref, carry_ref, qm_ref, *, tq, q_pos0, streams):
    tk = KEY_TILE
    pairs = SB_HEADS // 2
    pair_lanes = [slice(p * 2 * SB_HEAD_DIM, (p + 1) * 2 * SB_HEAD_DIM) for p in range(pairs)]
    q_first = q_pos0 + pl.program_id(1) * tq
    low_half = lax.broadcasted_iota(I32, (1, 2 * SB_HEAD_DIM), 1) < SB_HEAD_DIM

    def split_heads(x2):
        zero = jnp.zeros_like(x2)
        return jnp.concatenate([jnp.where(low_half, x2, zero), jnp.where(low_half, zero, x2)], axis=0)

    acc_ref[...] = jnp.zeros_like(acc_ref)
    carry_ref[...] = jnp.zeros_like(carry_ref)
    for b in range(streams):
        for p in range(pairs):
            qm_ref[b * pairs + p] = split_heads(-q_ref[b, :, pair_lanes[p]])

    def step(j, masked):
        k0 = pl.multiple_of(j * tk, tk)
        k_tiles = [k_ref[b, pl.ds(k0, tk), :] for b in range(streams)]
        neg = jnp.concatenate(
            [lax.dot_general(qm_ref[b * pairs + p], k_tiles[b][:, pair_lanes[p]], _NT, preferred_element_type=F32)
             for b in range(streams) for p in range(pairs)], axis=0).reshape(streams * SB_HEADS, tq, tk)
        soft = jnp.log(1.0 + jnp.exp(-jnp.abs(neg)))
        log_rest = jnp.minimum(neg, 0.0) - soft
        log_beta = log_rest - neg
        if masked:
            mask = ((k0 + lax.broadcasted_iota(I32, (tq, tk), 1))
                    < (q_first + lax.broadcasted_iota(I32, (tq, tk), 0)))[None]
            log_rest = jnp.where(mask, log_rest, 0.0)
        hi = log_rest.astype(BF16)
        lo = (log_rest - hi.astype(F32)).astype(BF16)
        sums = _dot(jnp.concatenate([hi, lo], axis=-1).reshape(streams * SB_HEADS * tq, 2 * tk), tri_ref[...])
        sums = sums.reshape(streams * SB_HEADS, tq, 2 * tk)
        carry = carry_ref[...]
        a = jnp.exp(log_beta + sums[:, :, :tk] + carry)
        if masked:
            a = jnp.where(mask, a, 0.0)
        a = a.astype(BF16)
        carry = carry + sums[:, :, tk:]
        carry_ref[...] = carry
        for b in range(streams):
            v_tile = v_ref[b, pl.ds(k0, tk), :]
            for p in range(pairs):
                head = b * SB_HEADS + 2 * p
                a2 = jnp.concatenate([a[head], a[head + 1]], axis=1)
                acc_ref[b, :, pair_lanes[p]] += _dot(a2, split_heads(v_tile[:, pair_lanes[p]]))
        return (jnp.max(carry) > -EXP_UNDERFLOW).astype(I32)

    j_first = (q_first + tq - 1) // tk
    go = step(j_first, True)
    lax.while_loop(lambda st: jnp.logical_and(st[0] >= 0, st[1] > 0),
                   lambda st: (st[0] - 1, step(st[0], False)), (j_first - 1, go))
    o_ref[...] = acc_ref[...].astype(BF16)


def _attn(qb, kb, vb, *, q_pos0):
    b, t, w = qb.shape
    s_k = kb.shape[1]
    tq = min(MAX_QUERY_TILE, t)
    tk = KEY_TILE
    g = ATTN_STREAMS if b % ATTN_STREAMS == 0 else 1
    assert t % tq == 0 and s_k % tk == 0 and s_k >= q_pos0 + t
    assert tk % tq == 0 and q_pos0 % tq == 0
    r = lax.broadcasted_iota(I32, (2 * tk, 2 * tk), 0) % tk
    c = lax.broadcasted_iota(I32, (2 * tk, 2 * tk), 1)
    tri = jnp.logical_or(r > c, c >= tk).astype(BF16)
    return pl.pallas_call(
        functools.partial(_attn_kernel, tq=tq, q_pos0=q_pos0, streams=g),
        grid=(b // g, t // tq),
        in_specs=[
            pl.BlockSpec((g, tq, w), lambda i, j: (i, j, 0)),
            pl.BlockSpec((g, s_k, w), lambda i, j: (i, 0, 0), pipeline_mode=pl.Buffered(1)),
            pl.BlockSpec((g, s_k, w), lambda i, j: (i, 0, 0), pipeline_mode=pl.Buffered(1)),
            pl.BlockSpec((2 * tk, 2 * tk), lambda i, j: (0, 0)),
        ],
        out_specs=pl.BlockSpec((g, tq, w), lambda i, j: (i, j, 0)),
        out_shape=jax.ShapeDtypeStruct((b, t, w), BF16),
        scratch_shapes=[pltpu.VMEM((g, tq, w), F32), pltpu.VMEM((g * SB_HEADS, tq, tk), F32),
                        pltpu.VMEM((g * SB_HEADS // 2, 2 * tq, 2 * SB_HEAD_DIM), BF16)],
        compiler_params=_params("parallel", "parallel"),
        name="attn",
    )(qb, kb, vb, tri)


def _merge_kernel(x_ref, osb_ref, u_ref, uprev_ref, hist_ref, sg_ref, g0_ref, b0_ref, wsb_ref, wgrp_ref,
                  pscale_ref, wpool_ref, wo_ref, g1_ref, b1_ref, h_ref, ext_ref, *, tm, hist_valid):
    t = pl.program_id(1)
    h0 = _layer_norm(x_ref[0], g0_ref[...], b0_ref[...])
    u = u_ref[0]
    ext_ref[:HALO, :] = jnp.where(t == 0, hist_ref[0], uprev_ref[0])
    ext_ref[HALO:, :] = u
    rows_seen = t * tm + lax.broadcasted_iota(I32, (tm, 1), 0) + (1 + hist_valid)
    pool_out = []
    for g, window in enumerate(POOL_WINDOWS):
        lanes = slice(g * POOL_GROUP_WIDTH, (g + 1) * POOL_GROUP_WIDTH)
        total = u[:, lanes]
        for back in range(1, window):
            total = total + ext_ref[HALO - back:HALO - back + tm, lanes]
        count = jnp.minimum(rows_seen, window).astype(F32)
        pooled = total / count - u[:, lanes]
        pool_out.append(_dot(pooled.astype(BF16), wgrp_ref[g]))
    o_pool = jnp.concatenate(pool_out, axis=1) * pscale_ref[...]
    a = _dot(osb_ref[0], wsb_ref[...])
    p = _dot(o_pool.astype(BF16), wpool_ref[...])
    sg = sg_ref[0].astype(F32)
    m = sg[:, :D_MODEL] * a + sg[:, D_MODEL:] * p
    mix = _dot(m.astype(BF16), wo_ref[...])
    h1 = _layer_norm(DEEPNORM_ALPHA * h0 + mix, g1_ref[...], b1_ref[...])
    h_ref[...] = _to_tiles(h1)


def _merge(x, osb, u, hist, sg, ln0_g, ln0_b, wsb, wgrp, pscale, wpool, wo, ln1_g, ln1_b, *, hist_valid):
    b, t, d = x.shape
    tm = min(TOKEN_TILE, t)
    nt = t // tm
    row = lambda width: pl.BlockSpec((1, tm, width), lambda i, j: (i, j, 0))
    const = lambda shape: pl.BlockSpec(shape, lambda i, j: (0,) * len(shape))
    prev = pl.BlockSpec((1, HALO, POOL_WIDTH), lambda i, j: (i, jnp.maximum(j * (tm // HALO) - 1, 0), 0))
    return pl.pallas_call(
        functools.partial(_merge_kernel, tm=tm, hist_valid=hist_valid),
        grid=(b, nt),
        in_specs=[
            row(d), row(SB_WIDTH), row(POOL_WIDTH), prev,
            pl.BlockSpec((1, HALO, POOL_WIDTH), lambda i, j: (i, 0, 0)),
            row(2 * d), const((1, d)), const((1, d)), const((SB_WIDTH, d)),
            const((len(POOL_WINDOWS), POOL_GROUP_WIDTH, POOL_GROUP_WIDTH)), const((1, POOL_WIDTH)),
            const((POOL_WIDTH, d)), const((d, d)), const((1, d)), const((1, d)),
        ],
        out_specs=pl.BlockSpec((tm,) + ROW_TILE, lambda i, j: (i * nt + j, 0, 0)),
        out_shape=jax.ShapeDtypeStruct((b * t,) + ROW_TILE, F32),
        scratch_shapes=[pltpu.VMEM((HALO + tm, POOL_WIDTH), F32)],
        compiler_params=_params("parallel", "parallel"),
        name="merge",
    )(x, osb, u, u, hist, sg, ln0_g.reshape(1, d), ln0_b.reshape(1, d), wsb, wgrp, pscale.reshape(1, -1),
      wpool, wo, ln1_g.reshape(1, d), ln1_b.reshape(1, d))


def _first_max(vals, index, n):
    top = jnp.max(vals, axis=0, keepdims=True)
    first = jnp.min(jnp.where(vals == top, index, n), axis=0, keepdims=True)
    return top, first


def _route_kernel(h_ref, wr_ref, bias_ref, tri_ref, cin_ref, idx_ref, gate_ref, rank_ref, cout_ref, *, tm):
    @pl.when(pl.program_id(0) == 0)
    def _():
        cout_ref[...] = cin_ref[...]

    h = _from_tiles(h_ref[...]).astype(BF16)
    scores = _sigmoid(lax.dot_general(wr_ref[...], h, _NT, preferred_element_type=F32))
    biased = scores + bias_ref[...]
    neg = jnp.float32(-jnp.inf)

    in_group = lax.broadcasted_iota(I32, (GROUP_SIZE, tm), 0)
    group_scores = []
    for g in range(N_EXPERT_GROUPS):
        blk = biased[g * GROUP_SIZE:(g + 1) * GROUP_SIZE, :]
        top, first = _first_max(blk, in_group, GROUP_SIZE)
        second = jnp.max(jnp.where(in_group == first, neg, blk), axis=0, keepdims=True)
        group_scores.append(top + second)
    remaining = jnp.concatenate(group_scores, axis=0)
    group_id = lax.broadcasted_iota(I32, (N_EXPERT_GROUPS, tm), 0)
    kept = jnp.zeros((N_EXPERT_GROUPS, tm), F32)
    for _ in range(TOPK_GROUPS):
        _, first = _first_max(remaining, group_id, N_EXPERT_GROUPS)
        hit = group_id == first
        kept = jnp.where(hit, 1.0, kept)
        remaining = jnp.where(hit, neg, remaining)
    remaining = jnp.concatenate(
        [jnp.where(kept[g:g + 1, :] > 0.5, biased[g * GROUP_SIZE:(g + 1) * GROUP_SIZE, :], neg)
         for g in range(N_EXPERT_GROUPS)], axis=0)

    expert_id = lax.broadcasted_iota(I32, (N_EXPERTS, tm), 0)
    chosen, picks, picked_scores = jnp.zeros((N_EXPERTS, tm), F32), [], []
    for _ in range(TOP_K):
        _, first = _first_max(remaining, expert_id, N_EXPERTS)
        hit = expert_id == first
        picks.append((first, hit))
        picked_scores.append(jnp.sum(jnp.where(hit, scores, 0.0), axis=0, keepdims=True))
        chosen = jnp.where(hit, 1.0, chosen)
        remaining = jnp.where(hit, neg, remaining)
    denom = picked_scores[0]
    for sc in picked_scores[1:]:
        denom = denom + sc
    gate_ref[...] = jnp.concatenate([sc / denom * ROUTED_SCALE for sc in picked_scores], axis=0)
    idx_ref[...] = jnp.concatenate([first for first, _ in picks], axis=0)

    slot = cout_ref[...] + _dot(chosen.astype(BF16), tri_ref[...])
    rank_ref[...] = jnp.concatenate(
        [jnp.sum(jnp.where(hit, slot, 0.0), axis=0, keepdims=True) for _, hit in picks], axis=0).astype(I32)
    cout_ref[...] += jnp.sum(chosen, axis=1, keepdims=True)


def _route(h, wr_t, bias, count_in):
    n = h.shape[0]
    tm = min(TOKEN_TILE, n)
    assert n % tm == 0
    earlier = (lax.broadcasted_iota(I32, (tm, tm), 0) < lax.broadcasted_iota(I32, (tm, tm), 1)).astype(BF16)
    const = lambda shape: pl.BlockSpec(shape, lambda i: (0,) * len(shape))
    picks = pl.BlockSpec((TOP_K, tm), lambda i: (0, i))
    return pl.pallas_call(
        functools.partial(_route_kernel, tm=tm),
        grid=(n // tm,),
        in_specs=[pl.BlockSpec((tm,) + ROW_TILE, lambda i: (i, 0, 0)), const((N_EXPERTS, D_MODEL)),
                  const((N_EXPERTS, 1)), const((tm, tm)), const((N_EXPERTS, 1))],
        out_specs=[picks, picks, picks, const((N_EXPERTS, 1))],
        out_shape=[jax.ShapeDtypeStruct((TOP_K, n), I32), jax.ShapeDtypeStruct((TOP_K, n), F32),
                   jax.ShapeDtypeStruct((TOP_K, n), I32), jax.ShapeDtypeStruct((N_EXPERTS, 1), F32)],
        compiler_params=_params("arbitrary"),
        name="route",
    )(h, wr_t, bias, earlier, count_in)


def _slot_kernel(idx_ref, rank_ref, start_ref, dest_ref, *, tm):
    expert_id = lax.broadcasted_iota(I32, (N_EXPERTS, tm), 0)
    start = start_ref[...]
    rows = [jnp.sum(jnp.where(expert_id == idx_ref[j:j + 1, :], start, 0.0), axis=0, keepdims=True)
            for j in range(TOP_K)]
    dest_ref[...] = jnp.concatenate(rows, axis=0).astype(I32) + rank_ref[...]


def _slots(idx, rank, start):
    n = idx.shape[1]
    tm = min(TOKEN_TILE, n)
    assert n % tm == 0
    picks = pl.BlockSpec((TOP_K, tm), lambda i: (0, i))
    return pl.pallas_call(
        functools.partial(_slot_kernel, tm=tm),
        grid=(n // tm,),
        in_specs=[picks, picks, pl.BlockSpec((N_EXPERTS, 1), lambda i: (0, 0))],
        out_specs=picks,
        out_shape=jax.ShapeDtypeStruct((TOP_K, n), I32),
        compiler_params=_params("parallel"),
        name="slots",
    )(idx, rank, start)


def _start_row_copies(tm, copy_of):
    def group(g, carry):
        tok0 = pl.multiple_of(g * SUBLANES, SUBLANES)
        for r in range(SUBLANES):
            for j in range(TOP_K):
                copy_of(tok0 + r, j, tok0 * TOP_K + (r * TOP_K + j)).start(priority=(r + j) % 2)
        return carry

    lax.fori_loop(0, tm // SUBLANES, group, 0)


def _dispatch_kernel(dest_ref, h_ref, *rest, tm):
    xs_ref, sem = rest[-2], rest[-1]
    _start_row_copies(
        tm, lambda tok, j, pair: pltpu.make_async_copy(h_ref.at[tok], xs_ref.at[dest_ref[pair]], sem))
    for _ in range(TOP_K):
        pltpu.make_async_copy(h_ref, xs_ref.at[pl.ds(0, tm)], sem).wait()


def _dispatch(dest, h, xs, cap):
    n = h.shape[0]
    tm = min(GATHER_TILE, n)
    assert n % tm == 0
    in_specs = [pl.BlockSpec((tm * TOP_K,), lambda i: (i,), memory_space=pltpu.SMEM),
                pl.BlockSpec((tm,) + ROW_TILE, lambda i: (i, 0, 0))]
    args = [dest, h]
    aliases = {}
    if xs is not None:
        in_specs.append(pl.BlockSpec(memory_space=pl.ANY))
        args.append(xs)
        aliases = {2: 0}
    return pl.pallas_call(
        functools.partial(_dispatch_kernel, tm=tm),
        grid=(n // tm,),
        in_specs=in_specs,
        out_specs=pl.BlockSpec(memory_space=pl.ANY),
        out_shape=jax.ShapeDtypeStruct((cap,) + ROW_TILE, F32),
        scratch_shapes=[pltpu.SemaphoreType.DMA],
        input_output_aliases=aliases,
        compiler_params=pltpu.CompilerParams(dimension_semantics=("arbitrary",), vmem_limit_bytes=VMEM_LIMIT,
                                             has_side_effects=True),
        name="dispatch",
    )(*args)


def _expert_kernel(first_ref, nblk_ref, used_ref, wg_ref, wu_ref, wd_ref, xs_ref, ys_ref,
                   wg_b, wu_b, wd_b, x_buf, y_buf, x_sem, y_sem):
    e = pl.program_id(0)
    blk, nx, ny, ahead = EXPERT_BLOCK, EXPERT_X_SLOTS, EXPERT_Y_SLOTS, EXPERT_X_AHEAD
    first, n, used = first_ref[e], nblk_ref[e], used_ref[0]

    def x_copy(g):
        slot = g % nx
        return pltpu.make_async_copy(xs_ref.at[pl.ds(pl.multiple_of(g * blk, blk), blk)], x_buf.at[slot],
                                     x_sem.at[slot])

    def y_copy(g):
        slot = g % ny
        return pltpu.make_async_copy(y_buf.at[slot], ys_ref.at[pl.ds(pl.multiple_of(g * blk, blk), blk)],
                                     y_sem.at[slot])

    @pl.when(e == 0)
    def _():
        for g in range(ahead):
            @pl.when(g < used)
            def _():
                x_copy(g).start()

    def compute(blocks):
        for g in blocks:
            @pl.when(g + ahead < used)
            def _():
                x_copy(g + ahead).start()
        for g in blocks:
            x_copy(g).wait()
        x = jnp.concatenate([_from_tiles(x_buf[g % nx]) for g in blocks], axis=0).astype(BF16)
        gate = _dot(x, wg_b[...])
        hid = gate * _sigmoid(gate) * _dot(x, wu_b[...])
        y = _dot(hid.astype(BF16), wd_b[...])
        for k, g in enumerate(blocks):
            @pl.when(g >= ny)
            def _():
                y_copy(g - ny).wait()

            y_buf[g % ny] = _to_tiles(y[k * blk:(k + 1) * blk])
            y_copy(g).start()

    @pl.when(n > 0)
    def _():
        wg_b[...] = wg_ref[0].astype(BF16)
        wu_b[...] = wu_ref[0].astype(BF16)
        wd_b[...] = wd_ref[0].astype(BF16)

        def pair(p, carry):
            compute([first + 2 * p, first + 2 * p + 1])
            return carry

        lax.fori_loop(0, n // 2, pair, 0)

        @pl.when(n % 2 == 1)
        def _():
            compute([first + n - 1])

    @pl.when(e == pl.num_programs(0) - 1)
    def _():
        for back in range(ny, 0, -1):
            @pl.when(used >= back)
            def _():
                y_copy(used - back).wait()


def _experts(first, nblk, used, xs, w_gate, w_up, w_down):
    cap = xs.shape[0]
    blk = EXPERT_BLOCK
    gate_up = pl.BlockSpec((1, D_MODEL, EXPERT_DIM), lambda e, *_: (e, 0, 0))
    return pl.pallas_call(
        _expert_kernel,
        grid_spec=pltpu.PrefetchScalarGridSpec(
            num_scalar_prefetch=3,
            grid=(N_EXPERTS,),
            in_specs=[gate_up, gate_up, pl.BlockSpec((1, EXPERT_DIM, D_MODEL), lambda e, *_: (e, 0, 0)),
                      pl.BlockSpec(memory_space=pl.ANY)],
            out_specs=pl.BlockSpec(memory_space=pl.ANY),
            scratch_shapes=[pltpu.VMEM((D_MODEL, EXPERT_DIM), BF16), pltpu.VMEM((D_MODEL, EXPERT_DIM), BF16),
                            pltpu.VMEM((EXPERT_DIM, D_MODEL), BF16),
                            pltpu.VMEM((EXPERT_X_SLOTS, blk) + ROW_TILE, F32),
                            pltpu.VMEM((EXPERT_Y_SLOTS, blk) + ROW_TILE, F32),
                            pltpu.SemaphoreType.DMA((EXPERT_X_SLOTS,)), pltpu.SemaphoreType.DMA((EXPERT_Y_SLOTS,))],
        ),
        out_shape=jax.ShapeDtypeStruct((cap,) + ROW_TILE, F32),
        compiler_params=_params("arbitrary"),
        name="experts",
    )(first, nblk, used, w_gate, w_up, w_down, xs)


def _combine_kernel(dest0_ref, dest1_ref, h_ref, gate_ref, ys_ref, wsg_ref, wsu_ref, wsd_ref, g_ref, b_ref, o_ref,
                    rows_ref, sem, *, tm):
    i = pl.program_id(0)
    slot = i % 2

    def gather(dest_ref, into):
        _start_row_copies(tm, lambda tok, j, pair: pltpu.make_async_copy(
            ys_ref.at[dest_ref[pair]], rows_ref.at[into, j, tok], sem.at[into]))

    @pl.when(i == 0)
    def _():
        gather(dest0_ref, 0)

    @pl.when(i + 1 < pl.num_programs(0))
    def _():
        gather(dest1_ref, 1 - slot)

    h = _from_tiles(h_ref[...])
    hb = h.astype(BF16)
    gate = _dot(hb, wsg_ref[...])
    hid = gate * _sigmoid(gate) * _dot(hb, wsu_ref[...])
    ffn = _dot(hid.astype(BF16), wsd_ref[...])
    for j in range(TOP_K):
        pltpu.make_async_copy(ys_ref.at[pl.ds(0, tm)], rows_ref.at[slot, j], sem.at[slot]).wait()
    weights = gate_ref[...]
    for j in range(TOP_K):
        ffn = ffn + weights[:, j:j + 1] * _from_tiles(rows_ref[slot, j])
    o_ref[...] = _layer_norm(DEEPNORM_ALPHA * h + ffn, g_ref[...], b_ref[...])


def _combine(dest, h, gate_t, ys, wsg, wsu, wsd, ln_g, ln_b):
    n, d = h.shape[0], D_MODEL
    tm = min(GATHER_TILE, n)
    assert n % tm == 0
    const = lambda shape: pl.BlockSpec(shape, lambda i: (0,) * len(shape))
    steps = n // tm
    return pl.pallas_call(
        functools.partial(_combine_kernel, tm=tm),
        grid=(steps,),
        in_specs=[pl.BlockSpec((tm * TOP_K,), lambda i: (0,), memory_space=pltpu.SMEM),
                  pl.BlockSpec((tm * TOP_K,), lambda i: (jnp.minimum(i + 1, steps - 1),), memory_space=pltpu.SMEM),
                  pl.BlockSpec((tm,) + ROW_TILE, lambda i: (i, 0, 0)),
                  pl.BlockSpec((tm, TOP_K), lambda i: (i, 0)),
                  pl.BlockSpec(memory_space=pl.ANY),
                  const((d, EXPERT_DIM)), const((d, EXPERT_DIM)), const((EXPERT_DIM, d)),
                  const((1, d)), const((1, d))],
        out_specs=pl.BlockSpec((tm, d), lambda i: (i, 0)),
        out_shape=jax.ShapeDtypeStruct((n, d), F32),
        scratch_shapes=[pltpu.VMEM((2, TOP_K, tm) + ROW_TILE, F32), pltpu.SemaphoreType.DMA((2,))],
        compiler_params=_params("arbitrary"),
        name="combine",
    )(dest, dest, h, gate_t, ys, wsg, wsu, wsd, ln_g.reshape(1, d), ln_b.reshape(1, d))


def _key_major(cache):
    b, h, s, hd = cache.shape
    return cache.transpose(0, 2, 1, 3).reshape(b, s, h * hd).astype(BF16)


def kernel(x_prompt, x_sample, cache_k, cache_v, state_pool, ln0_g, ln0_b, w_in, w_sb_up, w_pool_grp, pool_scale,
           w_pool_up, w_o, ln1_g, ln1_b, w_router, router_bias, w_exp_gate, w_exp_up, w_exp_down, w_sh_gate,
           w_sh_up, w_sh_down, ln2_g, ln2_b):
    assert w_in.shape[0] == DEPTH
    b_p, t_p, d = x_prompt.shape
    b_s, t_s, _ = x_sample.shape
    past = cache_k.shape[3]
    n_p, n_s = b_p * t_p, b_s * t_s
    layer = 0
    w_in_b = w_in[layer].astype(BF16)
    mix_w = (w_sb_up[layer].astype(BF16), w_pool_grp[layer].astype(BF16), pool_scale[layer],
             w_pool_up[layer].astype(BF16), w_o[layer].astype(BF16), ln1_g[layer], ln1_b[layer])

    qb, kb, vb, k_p, v_p, u_p, sg_p = _proj(x_prompt, ln0_g, ln0_b, w_in_b)
    osb_p = _attn(qb, kb, vb, q_pos0=0)
    hist_p = jnp.zeros((b_p, HALO, POOL_WIDTH), F32)
    h_p = _merge(x_prompt, osb_p, u_p, hist_p, sg_p, ln0_g, ln0_b, *mix_w, hist_valid=0)

    qs, ks, vs, k_s, v_s, u_s, sg_s = _proj(x_sample, ln0_g, ln0_b, w_in_b)
    key_rows = -(-(past + t_s) // KEY_TILE) * KEY_TILE
    pad = ((0, 0), (0, key_rows - past - t_s), (0, 0))
    k_all = jnp.pad(jnp.concatenate([_key_major(cache_k[layer]), ks], axis=1), pad)
    v_all = jnp.pad(jnp.concatenate([_key_major(cache_v[layer]), vs], axis=1), pad)
    osb_s = _attn(qs, k_all, v_all, q_pos0=past)
    hist_s = jnp.concatenate([jnp.zeros((b_s, 1, POOL_WIDTH), F32), state_pool[layer]], axis=1)
    h_s = _merge(x_sample, osb_s, u_s, hist_s, sg_s, ln0_g, ln0_b, *mix_w, hist_valid=POOL_STATE)

    wr_t = w_router[layer].T.astype(BF16)
    bias = router_bias[layer].astype(F32).reshape(N_EXPERTS, 1)
    idx_p, gate_p, rank_p, count = _route(h_p, wr_t, bias, jnp.zeros((N_EXPERTS, 1), F32))
    idx_s, gate_s, rank_s, count = _route(h_s, wr_t, bias, count)

    blk = EXPERT_BLOCK
    n_blocks = ((n_p + n_s) * TOP_K + N_EXPERTS * (blk - 1) + blk - 1) // blk
    cap = n_blocks * blk
    assert cap < 2 ** 24
    nblk = (count[:, 0].astype(I32) + blk - 1) // blk
    first = jnp.cumsum(nblk) - nblk
    used = jnp.sum(nblk).reshape(1)
    start = (first * blk).astype(F32).reshape(N_EXPERTS, 1)
    dest_p = _slots(idx_p, rank_p, start).T.reshape(-1)
    dest_s = _slots(idx_s, rank_s, start).T.reshape(-1)

    xs = _dispatch(dest_p, h_p, None, cap)
    xs = _dispatch(dest_s, h_s, xs, cap)
    ys = _experts(first, nblk, used, xs, w_exp_gate[layer], w_exp_up[layer], w_exp_down[layer])

    shared_w = (w_sh_gate[layer].astype(BF16), w_sh_up[layer].astype(BF16), w_sh_down[layer].astype(BF16),
                ln2_g[layer], ln2_b[layer])
    y_p = _combine(dest_p, h_p, gate_p.T, ys, *shared_w)
    y_s = _combine(dest_s, h_s, gate_s.T, ys, *shared_w)

    new_pool_p = u_p[:, t_p - POOL_STATE:][None]
    new_pool_s = jnp.concatenate([state_pool[layer], u_s], axis=1)[:, -POOL_STATE:][None]
    return (y_p.reshape(b_p, t_p, d), y_s.reshape(b_s, t_s, d), k_p, v_p, new_pool_p, k_s, v_s, new_pool_s)
```

```python
import functools

import jax
import jax.numpy as jnp
from jax import lax
from jax.experimental import pallas as pl
from jax.experimental.pallas import tpu as pltpu

F32 = jnp.float32
BF16 = jnp.bfloat16
I32 = jnp.int32

D_MODEL = 1024
SB_HEADS = 8
SB_HEAD_DIM = 64
SB_WIDTH = SB_HEADS * SB_HEAD_DIM
SB_SCALE = SB_HEAD_DIM ** -0.5
POOL_WINDOWS = (2, 4, 8, 16)
POOL_WIDTH = D_MODEL // 2
POOL_GROUP_WIDTH = POOL_WIDTH // len(POOL_WINDOWS)
POOL_STATE = max(POOL_WINDOWS) - 1
HALO = POOL_STATE + 1
N_EXPERTS = 256
TOP_K = 8
N_EXPERT_GROUPS = 8
TOPK_GROUPS = 4
GROUP_SIZE = N_EXPERTS // N_EXPERT_GROUPS
EXPERT_DIM = D_MODEL // 4
ROUTED_SCALE = 2.5
LN_EPS = 1e-5
DEPTH = 1
DEEPNORM_ALPHA = (2 * DEPTH) ** 0.25

SUBLANES, LANES = 8, 128
ROW_TILE = (SUBLANES, LANES)
assert SUBLANES * LANES == D_MODEL
EXPERT_X_AHEAD = 5
EXPERT_X_SLOTS = EXPERT_X_AHEAD + 2
EXPERT_Y_SLOTS = 4
KEY_TILE = 128
MAX_QUERY_TILE = 128
ATTN_STREAMS = 4
EXPERT_BLOCK = 256
TOKEN_TILE = 512
SCATTER_TILE = 2048
GATHER_TILE = 256
EXP_UNDERFLOW = 104.0
VMEM_LIMIT = 56 * 1024 * 1024

_NT = (((1,), (1,)), ((), ()))


def _params(*semantics):
    return pltpu.CompilerParams(dimension_semantics=semantics, vmem_limit_bytes=VMEM_LIMIT)


def _layer_norm(x, g, b):
    mu = jnp.mean(x, axis=-1, keepdims=True)
    xc = x - mu
    var = jnp.mean(xc * xc, axis=-1, keepdims=True)
    return xc * lax.rsqrt(var + LN_EPS) * g + b


def _sigmoid(x):
    return 1.0 / (1.0 + jnp.exp(-x))


def _dot(a, b):
    return jnp.dot(a, b, preferred_element_type=F32)


def _to_tiles(x):
    return x.reshape((x.shape[0],) + ROW_TILE)


def _from_tiles(x):
    return x.reshape(x.shape[0], D_MODEL)


def _proj_kernel(x_ref, g_ref, b_ref, w_ref, qb_ref, kb_ref, vb_ref, k_ref, v_ref, u_ref, sg_ref):
    h = _layer_norm(x_ref[0], g_ref[...], b_ref[...]).astype(BF16)

    def proj(lo, hi):
        return _dot(h, w_ref[:, lo:hi])

    qb_ref[0] = (proj(0, SB_WIDTH) * SB_SCALE).astype(BF16)
    zk = proj(SB_WIDTH, 2 * SB_WIDTH)
    kb_ref[0] = zk.astype(BF16)
    zv = proj(2 * SB_WIDTH, 3 * SB_WIDTH)
    vb_ref[0] = zv.astype(BF16)
    for hd in range(SB_HEADS):
        sl = slice(hd * SB_HEAD_DIM, (hd + 1) * SB_HEAD_DIM)
        k_ref[0, 0, hd] = zk[:, sl]
        v_ref[0, 0, hd] = zv[:, sl]
    u0 = 3 * SB_WIDTH
    u_ref[0] = proj(u0, u0 + POOL_WIDTH)
    g0 = u0 + POOL_WIDTH
    for c in range(2 * D_MODEL // SB_WIDTH):
        sl = slice(c * SB_WIDTH, (c + 1) * SB_WIDTH)
        sg_ref[0, :, sl] = _sigmoid(proj(g0 + c * SB_WIDTH, g0 + (c + 1) * SB_WIDTH)).astype(BF16)


def _proj(x, ln_g, ln_b, w_in_b):
    b, t, d = x.shape
    tm = min(TOKEN_TILE, t)
    in_width = w_in_b.shape[1]
    row = lambda width: pl.BlockSpec((1, tm, width), lambda i, j: (i, j, 0))
    heads = pl.BlockSpec((1, 1, SB_HEADS, tm, SB_HEAD_DIM), lambda i, j: (0, i, 0, j, 0))
    const = lambda shape: pl.BlockSpec(shape, lambda i, j: (0,) * len(shape))
    head_shape = jax.ShapeDtypeStruct((1, b, SB_HEADS, t, SB_HEAD_DIM), F32)
    return pl.pallas_call(
        _proj_kernel,
        grid=(b, t // tm),
        in_specs=[row(d), const((1, d)), const((1, d)), const((d, in_width))],
        out_specs=[row(SB_WIDTH), row(SB_WIDTH), row(SB_WIDTH), heads, heads, row(POOL_WIDTH), row(2 * d)],
        out_shape=[
            jax.ShapeDtypeStruct((b, t, SB_WIDTH), BF16),
            jax.ShapeDtypeStruct((b, t, SB_WIDTH), BF16),
            jax.ShapeDtypeStruct((b, t, SB_WIDTH), BF16),
            head_shape,
            head_shape,
            jax.ShapeDtypeStruct((b, t, POOL_WIDTH), F32),
            jax.ShapeDtypeStruct((b, t, 2 * d), BF16),
        ],
        compiler_params=_params("parallel", "parallel"),
        name="proj",
    )(x, ln_g.reshape(1, d), ln_b.reshape(1, d), w_in_b)


def _attn_kernel(q_ref, k_ref, v_ref, tri_ref, o_ref, acc_ref, carry_ref, qm_ref, *, tq, q_pos0, streams):
    tk = KEY_TILE
    pairs = SB_HEADS // 2
    pair_lanes = [slice(p * 2 * SB_HEAD_DIM, (p + 1) * 2 * SB_HEAD_DIM) for p in range(pairs)]
    q_first = q_pos0 + pl.program_id(1) * tq
    low_half = lax.broadcasted_iota(I32, (1, 2 * SB_HEAD_DIM), 1) < SB_HEAD_DIM

    def split_heads(x2):
        zero = jnp.zeros_like(x2)
        return jnp.concatenate([jnp.where(low_half, x2, zero), jnp.where(low_half, zero, x2)], axis=0)

    acc_ref[...] = jnp.zeros_like(acc_ref)
    carry_ref[...] = jnp.zeros_like(carry_ref)
    for b in range(streams):
        for p in range(pairs):
            qm_ref[b * pairs + p] = split_heads(-q_ref[b, :, pair_lanes[p]])

    def step(j, masked):
        k0 = pl.multiple_of(j * tk, tk)
        k_tiles = [k_ref[b, pl.ds(k0, tk), :] for b in range(streams)]
        neg = jnp.concatenate(
            [lax.dot_general(qm_ref[b * pairs + p], k_tiles[b][:, pair_lanes[p]], _NT, preferred_element_type=F32)
             for b in range(streams) for p in range(pairs)], axis=0).reshape(streams * SB_HEADS, tq, tk)
        soft = jnp.log(1.0 + jnp.exp(-jnp.abs(neg)))
        log_rest = jnp.minimum(neg, 0.0) - soft
        log_beta = log_rest - neg
        if masked:
            mask = ((k0 + lax.broadcasted_iota(I32, (tq, tk), 1))
                    < (q_first + lax.broadcasted_iota(I32, (tq, tk), 0)))[None]
            log_rest = jnp.where(mask, log_rest, 0.0)
        hi = log_rest.astype(BF16)
        lo = (log_rest - hi.astype(F32)).astype(BF16)
        sums = _dot(jnp.concatenate([hi, lo], axis=-1).reshape(streams * SB_HEADS * tq, 2 * tk), tri_ref[...])
        sums = sums.reshape(streams * SB_HEADS, tq, 2 * tk)
        carry = carry_ref[...]
        a = jnp.exp(log_beta + sums[:, :, :tk] + carry)
        if masked:
            a = jnp.where(mask, a, 0.0)
        a = a.astype(BF16)
        carry = carry + sums[:, :, tk:]
        carry_ref[...] = carry
        for b in range(streams):
            v_tile = v_ref[b, pl.ds(k0, tk), :]
            for p in range(pairs):
                head = b * SB_HEADS + 2 * p
                a2 = jnp.concatenate([a[head], a[head + 1]], axis=1)
                acc_ref[b, :, pair_lanes[p]] += _dot(a2, split_heads(v_tile[:, pair_lanes[p]]))
        return (jnp.max(carry) > -EXP_UNDERFLOW).astype(I32)

    j_first = (q_first + tq - 1) // tk
    go = step(j_first, True)
    lax.while_loop(lambda st: jnp.logical_and(st[0] >= 0, st[1] > 0),
                   lambda st: (st[0] - 1, step(st[0], False)), (j_first - 1, go))
    o_ref[...] = acc_ref[...].astype(BF16)


def _attn(qb, kb, vb, *, q_pos0):
    b, t, w = qb.shape
    s_k = kb.shape[1]
    tq = min(MAX_QUERY_TILE, t)
    tk = KEY_TILE
    g = ATTN_STREAMS if b % ATTN_STREAMS == 0 else 1
    assert t % tq == 0 and s_k % tk == 0 and s_k >= q_pos0 + t
    assert tk % tq == 0 and q_pos0 % tq == 0
    r = lax.broadcasted_iota(I32, (2 * tk, 2 * tk), 0) % tk
    c = lax.broadcasted_iota(I32, (2 * tk, 2 * tk), 1)
    tri = jnp.logical_or(r > c, c >= tk).astype(BF16)
    return pl.pallas_call(
        functools.partial(_attn_kernel, tq=tq, q_pos0=q_pos0, streams=g),
        grid=(b // g, t // tq),
        in_specs=[
            pl.BlockSpec((g, tq, w), lambda i, j: (i, j, 0)),
            pl.BlockSpec((g, s_k, w), lambda i, j: (i, 0, 0), pipeline_mode=pl.Buffered(1)),
            pl.BlockSpec((g, s_k, w), lambda i, j: (i, 0, 0), pipeline_mode=pl.Buffered(1)),
            pl.BlockSpec((2 * tk, 2 * tk), lambda i, j: (0, 0)),
        ],
        out_specs=pl.BlockSpec((g, tq, w), lambda i, j: (i, j, 0)),
        out_shape=jax.ShapeDtypeStruct((b, t, w), BF16),
        scratch_shapes=[pltpu.VMEM((g, tq, w), F32), pltpu.VMEM((g * SB_HEADS, tq, tk), F32),
                        pltpu.VMEM((g * SB_HEADS // 2, 2 * tq, 2 * SB_HEAD_DIM), BF16)],
        compiler_params=_params("parallel", "parallel"),
        name="attn",
    )(qb, kb, vb, tri)


def _merge_kernel(x_ref, osb_ref, u_ref, uprev_ref, hist_ref, sg_ref, g0_ref, b0_ref, wsb_ref, wgrp_ref,
                  pscale_ref, wpool_ref, wo_ref, g1_ref, b1_ref, h_ref, ext_ref, *, tm, hist_valid):
    t = pl.program_id(1)
    h0 = _layer_norm(x_ref[0], g0_ref[...], b0_ref[...])
    u = u_ref[0]
    ext_ref[:HALO, :] = jnp.where(t == 0, hist_ref[0], uprev_ref[0])
    ext_ref[HALO:, :] = u
    rows_seen = t * tm + lax.broadcasted_iota(I32, (tm, 1), 0) + (1 + hist_valid)
    pool_out = []
    for g, window in enumerate(POOL_WINDOWS):
        lanes = slice(g * POOL_GROUP_WIDTH, (g + 1) * POOL_GROUP_WIDTH)
        total = u[:, lanes]
        for back in range(1, window):
            total = total + ext_ref[HALO - back:HALO - back + tm, lanes]
        count = jnp.minimum(rows_seen, window).astype(F32)
        pooled = total / count - u[:, lanes]
        pool_out.append(_dot(pooled.astype(BF16), wgrp_ref[g]))
    o_pool = jnp.concatenate(pool_out, axis=1) * pscale_ref[...]
    a = _dot(osb_ref[0], wsb_ref[...])
    p = _dot(o_pool.astype(BF16), wpool_ref[...])
    sg = sg_ref[0].astype(F32)
    m = sg[:, :D_MODEL] * a + sg[:, D_MODEL:] * p
    mix = _dot(m.astype(BF16), wo_ref[...])
    h1 = _layer_norm(DEEPNORM_ALPHA * h0 + mix, g1_ref[...], b1_ref[...])
    h_ref[...] = _to_tiles(h1)


def _merge(x, osb, u, hist, sg, ln0_g, ln0_b, wsb, wgrp, pscale, wpool, wo, ln1_g, ln1_b, *, hist_valid):
    b, t, d = x.shape
    tm = min(TOKEN_TILE, t)
    nt = t // tm
    row = lambda width: pl.BlockSpec((1, tm, width), lambda i, j: (i, j, 0))
    const = lambda shape: pl.BlockSpec(shape, lambda i, j: (0,) * len(shape))
    prev = pl.BlockSpec((1, HALO, POOL_WIDTH), lambda i, j: (i, jnp.maximum(j * (tm // HALO) - 1, 0), 0))
    return pl.pallas_call(
        functools.partial(_merge_kernel, tm=tm, hist_valid=hist_valid),
        grid=(b, nt),
        in_specs=[
            row(d), row(SB_WIDTH), row(POOL_WIDTH), prev,
            pl.BlockSpec((1, HALO, POOL_WIDTH), lambda i, j: (i, 0, 0)),
            row(2 * d), const((1, d)), const((1, d)), const((SB_WIDTH, d)),
            const((len(POOL_WINDOWS), POOL_GROUP_WIDTH, POOL_GROUP_WIDTH)), const((1, POOL_WIDTH)),
            const((POOL_WIDTH, d)), const((d, d)), const((1, d)), const((1, d)),
        ],
        out_specs=pl.BlockSpec((tm,) + ROW_TILE, lambda i, j: (i * nt + j, 0, 0)),
        out_shape=jax.ShapeDtypeStruct((b * t,) + ROW_TILE, F32),
        scratch_shapes=[pltpu.VMEM((HALO + tm, POOL_WIDTH), F32)],
        compiler_params=_params("parallel", "parallel"),
        name="merge",
    )(x, osb, u, u, hist, sg, ln0_g.reshape(1, d), ln0_b.reshape(1, d), wsb, wgrp, pscale.reshape(1, -1),
      wpool, wo, ln1_g.reshape(1, d), ln1_b.reshape(1, d))


def _first_max(vals, index, n):
    top = jnp.max(vals, axis=0, keepdims=True)
    first = jnp.min(jnp.where(vals == top, index, n), axis=0, keepdims=True)
    return top, first


def _route_kernel(h_ref, wr_ref, bias_ref, tri_ref, cin_ref, idx_ref, gate_ref, rank_ref, cout_ref, *, tm):
    @pl.when(pl.program_id(0) == 0)
    def _():
        cout_ref[...] = cin_ref[...]

    h = _from_tiles(h_ref[...]).astype(BF16)
    scores = _sigmoid(lax.dot_general(wr_ref[...], h, _NT, preferred_element_type=F32))
    biased = scores + bias_ref[...]
    neg = jnp.float32(-jnp.inf)

    in_group = lax.broadcasted_iota(I32, (GROUP_SIZE, tm), 0)
    group_scores = []
    for g in range(N_EXPERT_GROUPS):
        blk = biased[g * GROUP_SIZE:(g + 1) * GROUP_SIZE, :]
        top, first = _first_max(blk, in_group, GROUP_SIZE)
        second = jnp.max(jnp.where(in_group == first, neg, blk), axis=0, keepdims=True)
        group_scores.append(top + second)
    remaining = jnp.concatenate(group_scores, axis=0)
    group_id = lax.broadcasted_iota(I32, (N_EXPERT_GROUPS, tm), 0)
    kept = jnp.zeros((N_EXPERT_GROUPS, tm), F32)
    for _ in range(TOPK_GROUPS):
        _, first = _first_max(remaining, group_id, N_EXPERT_GROUPS)
        hit = group_id == first
        kept = jnp.where(hit, 1.0, kept)
        remaining = jnp.where(hit, neg, remaining)
    remaining = jnp.concatenate(
        [jnp.where(kept[g:g + 1, :] > 0.5, biased[g * GROUP_SIZE:(g + 1) * GROUP_SIZE, :], neg)
         for g in range(N_EXPERT_GROUPS)], axis=0)

    expert_id = lax.broadcasted_iota(I32, (N_EXPERTS, tm), 0)
    chosen, picks, picked_scores = jnp.zeros((N_EXPERTS, tm), F32), [], []
    for _ in range(TOP_K):
        _, first = _first_max(remaining, expert_id, N_EXPERTS)
        hit = expert_id == first
        picks.append((first, hit))
        picked_scores.append(jnp.sum(jnp.where(hit, scores, 0.0), axis=0, keepdims=True))
        chosen = jnp.where(hit, 1.0, chosen)
        remaining = jnp.where(hit, neg, remaining)
    denom = picked_scores[0]
    for sc in picked_scores[1:]:
        denom = denom + sc
    gate_ref[...] = jnp.concatenate([sc / denom * ROUTED_SCALE for sc in picked_scores], axis=0)
    idx_ref[...] = jnp.concatenate([first for first, _ in picks], axis=0)

    slot = cout_ref[...] + _dot(chosen.astype(BF16), tri_ref[...])
    rank_ref[...] = jnp.concatenate(
        [jnp.sum(jnp.where(hit, slot, 0.0), axis=0, keepdims=True) for _, hit in picks], axis=0).astype(I32)
    cout_ref[...] += jnp.sum(chosen, axis=1, keepdims=True)


def _route(h, wr_t, bias, count_in):
    n = h.shape[0]
    tm = min(TOKEN_TILE, n)
    assert n % tm == 0
    earlier = (lax.broadcasted_iota(I32, (tm, tm), 0) < lax.broadcasted_iota(I32, (tm, tm), 1)).astype(BF16)
    const = lambda shape: pl.BlockSpec(shape, lambda i: (0,) * len(shape))
    picks = pl.BlockSpec((TOP_K, tm), lambda i: (0, i))
    return pl.pallas_call(
        functools.partial(_route_kernel, tm=tm),
        grid=(n // tm,),
        in_specs=[pl.BlockSpec((tm,) + ROW_TILE, lambda i: (i, 0, 0)), const((N_EXPERTS, D_MODEL)),
                  const((N_EXPERTS, 1)), const((tm, tm)), const((N_EXPERTS, 1))],
        out_specs=[picks, picks, picks, const((N_EXPERTS, 1))],
        out_shape=[jax.ShapeDtypeStruct((TOP_K, n), I32), jax.ShapeDtypeStruct((TOP_K, n), F32),
                   jax.ShapeDtypeStruct((TOP_K, n), I32), jax.ShapeDtypeStruct((N_EXPERTS, 1), F32)],
        compiler_params=_params("arbitrary"),
        name="route",
    )(h, wr_t, bias, earlier, count_in)


def _slot_kernel(idx_ref, rank_ref, start_ref, dest_ref, *, tm):
    expert_id = lax.broadcasted_iota(I32, (N_EXPERTS, tm), 0)
    start = start_ref[...]
    rows = [jnp.sum(jnp.where(expert_id == idx_ref[j:j + 1, :], start, 0.0), axis=0, keepdims=True)
            for j in range(TOP_K)]
    dest_ref[...] = jnp.concatenate(rows, axis=0).astype(I32) + rank_ref[...]


def _slots(idx, rank, start):
    n = idx.shape[1]
    tm = min(TOKEN_TILE, n)
    assert n % tm == 0
    picks = pl.BlockSpec((TOP_K, tm), lambda i: (0, i))
    return pl.pallas_call(
        functools.partial(_slot_kernel, tm=tm),
        grid=(n // tm,),
        in_specs=[picks, picks, pl.BlockSpec((N_EXPERTS, 1), lambda i: (0, 0))],
        out_specs=picks,
        out_shape=jax.ShapeDtypeStruct((TOP_K, n), I32),
        compiler_params=_params("parallel"),
        name="slots",
    )(idx, rank, start)


def _start_row_copies(tm, copy_of):
    def group(g, carry):
        tok0 = pl.multiple_of(g * SUBLANES, SUBLANES)
        for r in range(SUBLANES):
            for j in range(TOP_K):
                copy_of(tok0 + r, j, tok0 * TOP_K + (r * TOP_K + j)).start(priority=(r + j) % 2)
        return carry

    lax.fori_loop(0, tm // SUBLANES, group, 0)


def _dispatch_kernel(dest_ref, h_ref, *rest, tm):
    xs_ref, sem = rest[-2], rest[-1]
    _start_row_copies(
        tm, lambda tok, j, pair: pltpu.make_async_copy(h_ref.at[tok], xs_ref.at[dest_ref[pair]], sem))
    for _ in range(TOP_K):
        pltpu.make_async_copy(h_ref, xs_ref.at[pl.ds(0, tm)], sem).wait()


def _dispatch(dest, h, xs, cap):
    n = h.shape[0]
    tm = min(SCATTER_TILE, n)
    assert n % tm == 0
    in_specs = [pl.BlockSpec((tm * TOP_K,), lambda i: (i,), memory_space=pltpu.SMEM),
                pl.BlockSpec((tm,) + ROW_TILE, lambda i: (i, 0, 0))]
    args = [dest, h]
    aliases = {}
    if xs is not None:
        in_specs.append(pl.BlockSpec(memory_space=pl.ANY))
        args.append(xs)
        aliases = {2: 0}
    return pl.pallas_call(
        functools.partial(_dispatch_kernel, tm=tm),
        grid=(n // tm,),
        in_specs=in_specs,
        out_specs=pl.BlockSpec(memory_space=pl.ANY),
        out_shape=jax.ShapeDtypeStruct((cap,) + ROW_TILE, F32),
        scratch_shapes=[pltpu.SemaphoreType.DMA],
        input_output_aliases=aliases,
        compiler_params=pltpu.CompilerParams(dimension_semantics=("arbitrary",), vmem_limit_bytes=VMEM_LIMIT,
                                             has_side_effects=True),
        name="dispatch",
    )(*args)


def _expert_kernel(first_ref, nblk_ref, used_ref, wg_ref, wu_ref, wd_ref, xs_ref, ys_ref,
                   wg_b, wu_b, wd_b, x_buf, y_buf, x_sem, y_sem):
    e = pl.program_id(0)
    blk, nx, ny, ahead = EXPERT_BLOCK, EXPERT_X_SLOTS, EXPERT_Y_SLOTS, EXPERT_X_AHEAD
    first, n, used = first_ref[e], nblk_ref[e], used_ref[0]

    def x_copy(g):
        slot = g % nx
        return pltpu.make_async_copy(xs_ref.at[pl.ds(pl.multiple_of(g * blk, blk), blk)], x_buf.at[slot],
                                     x_sem.at[slot])

    def y_copy(g):
        slot = g % ny
        return pltpu.make_async_copy(y_buf.at[slot], ys_ref.at[pl.ds(pl.multiple_of(g * blk, blk), blk)],
                                     y_sem.at[slot])

    @pl.when(e == 0)
    def _():
        for g in range(ahead):
            @pl.when(g < used)
            def _():
                x_copy(g).start()

    def compute(blocks):
        for g in blocks:
            @pl.when(g + ahead < used)
            def _():
                x_copy(g + ahead).start()
        for g in blocks:
            x_copy(g).wait()
        x = jnp.concatenate([_from_tiles(x_buf[g % nx]) for g in blocks], axis=0).astype(BF16)
        gate = _dot(x, wg_b[...])
        hid = gate * _sigmoid(gate) * _dot(x, wu_b[...])
        y = _dot(hid.astype(BF16), wd_b[...])
        for k, g in enumerate(blocks):
            @pl.when(g >= ny)
            def _():
                y_copy(g - ny).wait()

            y_buf[g % ny] = _to_tiles(y[k * blk:(k + 1) * blk])
            y_copy(g).start()

    @pl.when(n > 0)
    def _():
        wg_b[...] = wg_ref[0].astype(BF16)
        wu_b[...] = wu_ref[0].astype(BF16)
        wd_b[...] = wd_ref[0].astype(BF16)

        def pair(p, carry):
            compute([first + 2 * p, first + 2 * p + 1])
            return carry

        lax.fori_loop(0, n // 2, pair, 0)

        @pl.when(n % 2 == 1)
        def _():
            compute([first + n - 1])

    @pl.when(e == pl.num_programs(0) - 1)
    def _():
        for back in range(ny, 0, -1):
            @pl.when(used >= back)
            def _():
                y_copy(used - back).wait()


def _experts(first, nblk, used, xs, w_gate, w_up, w_down):
    cap = xs.shape[0]
    blk = EXPERT_BLOCK
    gate_up = pl.BlockSpec((1, D_MODEL, EXPERT_DIM), lambda e, *_: (e, 0, 0))
    return pl.pallas_call(
        _expert_kernel,
        grid_spec=pltpu.PrefetchScalarGridSpec(
            num_scalar_prefetch=3,
            grid=(N_EXPERTS,),
            in_specs=[gate_up, gate_up, pl.BlockSpec((1, EXPERT_DIM, D_MODEL), lambda e, *_: (e, 0, 0)),
                      pl.BlockSpec(memory_space=pl.ANY)],
            out_specs=pl.BlockSpec(memory_space=pl.ANY),
            scratch_shapes=[pltpu.VMEM((D_MODEL, EXPERT_DIM), BF16), pltpu.VMEM((D_MODEL, EXPERT_DIM), BF16),
                            pltpu.VMEM((EXPERT_DIM, D_MODEL), BF16),
                            pltpu.VMEM((EXPERT_X_SLOTS, blk) + ROW_TILE, F32),
                            pltpu.VMEM((EXPERT_Y_SLOTS, blk) + ROW_TILE, F32),
                            pltpu.SemaphoreType.DMA((EXPERT_X_SLOTS,)), pltpu.SemaphoreType.DMA((EXPERT_Y_SLOTS,))],
        ),
        out_shape=jax.ShapeDtypeStruct((cap,) + ROW_TILE, F32),
        compiler_params=_params("arbitrary"),
        name="experts",
    )(first, nblk, used, w_gate, w_up, w_down, xs)


def _combine_kernel(dest0_ref, dest1_ref, h_ref, gate_ref, ys_ref, wsg_ref, wsu_ref, wsd_ref, g_ref, b_ref, o_ref,
                    rows_ref, sem, *, tm):
    i = pl.program_id(0)
    slot = i % 2

    def gather(dest_ref, into):
        _start_row_copies(tm, lambda tok, j, pair: pltpu.make_async_copy(
            ys_ref.at[dest_ref[pair]], rows_ref.at[into, j, tok], sem.at[into]))

    @pl.when(i == 0)
    def _():
        gather(dest0_ref, 0)

    @pl.when(i + 1 < pl.num_programs(0))
    def _():
        gather(dest1_ref, 1 - slot)

    h = _from_tiles(h_ref[...])
    hb = h.astype(BF16)
    gate = _dot(hb, wsg_ref[...])
    hid = gate * _sigmoid(gate) * _dot(hb, wsu_ref[...])
    ffn = _dot(hid.astype(BF16), wsd_ref[...])
    for j in range(TOP_K):
        pltpu.make_async_copy(ys_ref.at[pl.ds(0, tm)], rows_ref.at[slot, j], sem.at[slot]).wait()
    weights = gate_ref[...]
    for j in range(TOP_K):
        ffn = ffn + weights[:, j:j + 1] * _from_tiles(rows_ref[slot, j])
    o_ref[...] = _layer_norm(DEEPNORM_ALPHA * h + ffn, g_ref[...], b_ref[...])


def _combine(dest, h, gate_t, ys, wsg, wsu, wsd, ln_g, ln_b):
    n, d = h.shape[0], D_MODEL
    tm = min(GATHER_TILE, n)
    assert n % tm == 0
    const = lambda shape: pl.BlockSpec(shape, lambda i: (0,) * len(shape))
    steps = n // tm
    return pl.pallas_call(
        functools.partial(_combine_kernel, tm=tm),
        grid=(steps,),
        in_specs=[pl.BlockSpec((tm * TOP_K,), lambda i: (0,), memory_space=pltpu.SMEM),
                  pl.BlockSpec((tm * TOP_K,), lambda i: (jnp.minimum(i + 1, steps - 1),), memory_space=pltpu.SMEM),
                  pl.BlockSpec((tm,) + ROW_TILE, lambda i: (i, 0, 0)),
                  pl.BlockSpec((tm, TOP_K), lambda i: (i, 0)),
                  pl.BlockSpec(memory_space=pl.ANY),
                  const((d, EXPERT_DIM)), const((d, EXPERT_DIM)), const((EXPERT_DIM, d)),
                  const((1, d)), const((1, d))],
        out_specs=pl.BlockSpec((tm, d), lambda i: (i, 0)),
        out_shape=jax.ShapeDtypeStruct((n, d), F32),
        scratch_shapes=[pltpu.VMEM((2, TOP_K, tm) + ROW_TILE, F32), pltpu.SemaphoreType.DMA((2,))],
        compiler_params=_params("arbitrary"),
        name="combine",
    )(dest, dest, h, gate_t, ys, wsg, wsu, wsd, ln_g.reshape(1, d), ln_b.reshape(1, d))


def _key_major(cache):
    b, h, s, hd = cache.shape
    return cache.transpose(0, 2, 1, 3).reshape(b, s, h * hd).astype(BF16)


def kernel(x_prompt, x_sample, cache_k, cache_v, state_pool, ln0_g, ln0_b, w_in, w_sb_up, w_pool_grp, pool_scale,
           w_pool_up, w_o, ln1_g, ln1_b, w_router, router_bias, w_exp_gate, w_exp_up, w_exp_down, w_sh_gate,
           w_sh_up, w_sh_down, ln2_g, ln2_b):
    assert w_in.shape[0] == DEPTH
    b_p, t_p, d = x_prompt.shape
    b_s, t_s, _ = x_sample.shape
    past = cache_k.shape[3]
    n_p, n_s = b_p * t_p, b_s * t_s
    layer = 0
    w_in_b = w_in[layer].astype(BF16)
    mix_w = (w_sb_up[layer].astype(BF16), w_pool_grp[layer].astype(BF16), pool_scale[layer],
             w_pool_up[layer].astype(BF16), w_o[layer].astype(BF16), ln1_g[layer], ln1_b[layer])

    qb, kb, vb, k_p, v_p, u_p, sg_p = _proj(x_prompt, ln0_g, ln0_b, w_in_b)
    osb_p = _attn(qb, kb, vb, q_pos0=0)
    hist_p = jnp.zeros((b_p, HALO, POOL_WIDTH), F32)
    h_p = _merge(x_prompt, osb_p, u_p, hist_p, sg_p, ln0_g, ln0_b, *mix_w, hist_valid=0)

    qs, ks, vs, k_s, v_s, u_s, sg_s = _proj(x_sample, ln0_g, ln0_b, w_in_b)
    key_rows = -(-(past + t_s) // KEY_TILE) * KEY_TILE
    pad = ((0, 0), (0, key_rows - past - t_s), (0, 0))
    k_all = jnp.pad(jnp.concatenate([_key_major(cache_k[layer]), ks], axis=1), pad)
    v_all = jnp.pad(jnp.concatenate([_key_major(cache_v[layer]), vs], axis=1), pad)
    osb_s = _attn(qs, k_all, v_all, q_pos0=past)
    hist_s = jnp.concatenate([jnp.zeros((b_s, 1, POOL_WIDTH), F32), state_pool[layer]], axis=1)
    h_s = _merge(x_sample, osb_s, u_s, hist_s, sg_s, ln0_g, ln0_b, *mix_w, hist_valid=POOL_STATE)

    wr_t = w_router[layer].T.astype(BF16)
    bias = router_bias[layer].astype(F32).reshape(N_EXPERTS, 1)
    idx_p, gate_p, rank_p, count = _route(h_p, wr_t, bias, jnp.zeros((N_EXPERTS, 1), F32))
    idx_s, gate_s, rank_s, count = _route(h_s, wr_t, bias, count)

    blk = EXPERT_BLOCK
    n_blocks = ((n_p + n_s) * TOP_K + N_EXPERTS * (blk - 1) + blk - 1) // blk
    cap = n_blocks * blk
    assert cap < 2 ** 24
    nblk = (count[:, 0].astype(I32) + blk - 1) // blk
    first = jnp.cumsum(nblk) - nblk
    used = jnp.sum(nblk).reshape(1)
    start = (first * blk).astype(F32).reshape(N_EXPERTS, 1)
    dest_p = _slots(idx_p, rank_p, start).T.reshape(-1)
    dest_s = _slots(idx_s, rank_s, start).T.reshape(-1)

    xs = _dispatch(dest_p, h_p, None, cap)
    xs = _dispatch(dest_s, h_s, xs, cap)
    ys = _experts(first, nblk, used, xs, w_exp_gate[layer], w_exp_up[layer], w_exp_down[layer])

    shared_w = (w_sh_gate[layer].astype(BF16), w_sh_up[layer].astype(BF16), w_sh_down[layer].astype(BF16),
                ln2_g[layer], ln2_b[layer])
    y_p = _combine(dest_p, h_p, gate_p.T, ys, *shared_w)
    y_s = _combine(dest_s, h_s, gate_s.T, ys, *shared_w)

    new_pool_p = u_p[:, t_p - POOL_STATE:][None]
    new_pool_s = jnp.concatenate([state_pool[layer], u_s], axis=1)[:, -POOL_STATE:][None]
    return (y_p.reshape(b_p, t_p, d), y_s.reshape(b_s, t_s, d), k_p, v_p, new_pool_p, k_s, v_s, new_pool_s)
```

```python
import functools

import jax
import jax.numpy as jnp
from jax import lax
from jax.experimental import pallas as pl
from jax.experimental.pallas import tpu as pltpu

F32 = jnp.float32
BF16 = jnp.bfloat16
I32 = jnp.int32

D_MODEL = 1024
SB_HEADS = 8
SB_HEAD_DIM = 64
SB_WIDTH = SB_HEADS * SB_HEAD_DIM
SB_SCALE = SB_HEAD_DIM ** -0.5
POOL_WINDOWS = (2, 4, 8, 16)
POOL_WIDTH = D_MODEL // 2
POOL_GROUP_WIDTH = POOL_WIDTH // len(POOL_WINDOWS)
POOL_STATE = max(POOL_WINDOWS) - 1
HALO = POOL_STATE + 1
N_EXPERTS = 256
TOP_K = 8
N_EXPERT_GROUPS = 8
TOPK_GROUPS = 4
GROUP_SIZE = N_EXPERTS // N_EXPERT_GROUPS
EXPERT_DIM = D_MODEL // 4
ROUTED_SCALE = 2.5
LN_EPS = 1e-5
DEPTH = 1
DEEPNORM_ALPHA = (2 * DEPTH) ** 0.25

SUBLANES, LANES = 8, 128
ROW_TILE = (SUBLANES, LANES)
assert SUBLANES * LANES == D_MODEL
EXPERT_X_AHEAD = 5
EXPERT_X_SLOTS = EXPERT_X_AHEAD + 2
EXPERT_Y_SLOTS = 4
KEY_TILE = 128
MAX_QUERY_TILE = 128
ATTN_STREAMS = 4
EXPERT_BLOCK = 256
TOKEN_TILE = 512
SCATTER_TILE = 2048
GATHER_TILE = 256
COMBINE_CHUNK = 64
EXP_UNDERFLOW = 104.0
VMEM_LIMIT = 56 * 1024 * 1024

_NT = (((1,), (1,)), ((), ()))


def _params(*semantics):
    return pltpu.CompilerParams(dimension_semantics=semantics, vmem_limit_bytes=VMEM_LIMIT)


def _layer_norm(x, g, b):
    mu = jnp.mean(x, axis=-1, keepdims=True)
    xc = x - mu
    var = jnp.mean(xc * xc, axis=-1, keepdims=True)
    return xc * lax.rsqrt(var + LN_EPS) * g + b


def _sigmoid(x):
    return 1.0 / (1.0 + jnp.exp(-x))


def _dot(a, b):
    return jnp.dot(a, b, preferred_element_type=F32)


def _to_tiles(x):
    return x.reshape((x.shape[0],) + ROW_TILE)


def _from_tiles(x):
    return x.reshape(x.shape[0], D_MODEL)


def _proj_kernel(x_ref, g_ref, b_ref, w_ref, qb_ref, kb_ref, vb_ref, k_ref, v_ref, u_ref, sg_ref):
    h = _layer_norm(x_ref[0], g_ref[...], b_ref[...]).astype(BF16)

    def proj(lo, hi):
        return _dot(h, w_ref[:, lo:hi])

    qb_ref[0] = (proj(0, SB_WIDTH) * SB_SCALE).astype(BF16)
    zk = proj(SB_WIDTH, 2 * SB_WIDTH)
    kb_ref[0] = zk.astype(BF16)
    zv = proj(2 * SB_WIDTH, 3 * SB_WIDTH)
    vb_ref[0] = zv.astype(BF16)
    for hd in range(SB_HEADS):
        sl = slice(hd * SB_HEAD_DIM, (hd + 1) * SB_HEAD_DIM)
        k_ref[0, 0, hd] = zk[:, sl]
        v_ref[0, 0, hd] = zv[:, sl]
    u0 = 3 * SB_WIDTH
    u_ref[0] = proj(u0, u0 + POOL_WIDTH)
    g0 = u0 + POOL_WIDTH
    for c in range(2 * D_MODEL // SB_WIDTH):
        sl = slice(c * SB_WIDTH, (c + 1) * SB_WIDTH)
        sg_ref[0, :, sl] = _sigmoid(proj(g0 + c * SB_WIDTH, g0 + (c + 1) * SB_WIDTH)).astype(BF16)


def _proj(x, ln_g, ln_b, w_in_b):
    b, t, d = x.shape
    tm = min(TOKEN_TILE, t)
    in_width = w_in_b.shape[1]
    row = lambda width: pl.BlockSpec((1, tm, width), lambda i, j: (i, j, 0))
    heads = pl.BlockSpec((1, 1, SB_HEADS, tm, SB_HEAD_DIM), lambda i, j: (0, i, 0, j, 0))
    const = lambda shape: pl.BlockSpec(shape, lambda i, j: (0,) * len(shape))
    head_shape = jax.ShapeDtypeStruct((1, b, SB_HEADS, t, SB_HEAD_DIM), F32)
    return pl.pallas_call(
        _proj_kernel,
        grid=(b, t // tm),
        in_specs=[row(d), const((1, d)), const((1, d)), const((d, in_width))],
        out_specs=[row(SB_WIDTH), row(SB_WIDTH), row(SB_WIDTH), heads, heads, row(POOL_WIDTH), row(2 * d)],
        out_shape=[
            jax.ShapeDtypeStruct((b, t, SB_WIDTH), BF16),
            jax.ShapeDtypeStruct((b, t, SB_WIDTH), BF16),
            jax.ShapeDtypeStruct((b, t, SB_WIDTH), BF16),
            head_shape,
            head_shape,
            jax.ShapeDtypeStruct((b, t, POOL_WIDTH), F32),
            jax.ShapeDtypeStruct((b, t, 2 * d), BF16),
        ],
        compiler_params=_params("parallel", "parallel"),
        name="proj",
    )(x, ln_g.reshape(1, d), ln_b.reshape(1, d), w_in_b)


def _attn_kernel(q_ref, k_ref, v_ref, tri_ref, o_ref, acc_ref, carry_ref, qm_ref, *, tq, q_pos0, streams):
    tk = KEY_TILE
    pairs = SB_HEADS // 2
    pair_lanes = [slice(p * 2 * SB_HEAD_DIM, (p + 1) * 2 * SB_HEAD_DIM) for p in range(pairs)]
    q_first = q_pos0 + pl.program_id(1) * tq
    low_half = lax.broadcasted_iota(I32, (1, 2 * SB_HEAD_DIM), 1) < SB_HEAD_DIM

    def split_heads(x2):
        zero = jnp.zeros_like(x2)
        return jnp.concatenate([jnp.where(low_half, x2, zero), jnp.where(low_half, zero, x2)], axis=0)

    acc_ref[...] = jnp.zeros_like(acc_ref)
    carry_ref[...] = jnp.zeros_like(carry_ref)
    for b in range(streams):
        for p in range(pairs):
            qm_ref[b * pairs + p] = split_heads(-q_ref[b, :, pair_lanes[p]])

    def step(j, masked):
        k0 = pl.multiple_of(j * tk, tk)
        k_tiles = [k_ref[b, pl.ds(k0, tk), :] for b in range(streams)]
        neg = jnp.concatenate(
            [lax.dot_general(qm_ref[b * pairs + p], k_tiles[b][:, pair_lanes[p]], _NT, preferred_element_type=F32)
             for b in range(streams) for p in range(pairs)], axis=0).reshape(streams * SB_HEADS, tq, tk)
        soft = jnp.log(1.0 + jnp.exp(-jnp.abs(neg)))
        log_rest = jnp.minimum(neg, 0.0) - soft
        log_beta = log_rest - neg
        if masked:
            mask = ((k0 + lax.broadcasted_iota(I32, (tq, tk), 1))
                    < (q_first + lax.broadcasted_iota(I32, (tq, tk), 0)))[None]
            log_rest = jnp.where(mask, log_rest, 0.0)
        hi = log_rest.astype(BF16)
        lo = (log_rest - hi.astype(F32)).astype(BF16)
        sums = _dot(jnp.concatenate([hi, lo], axis=-1).reshape(streams * SB_HEADS * tq, 2 * tk), tri_ref[...])
        sums = sums.reshape(streams * SB_HEADS, tq, 2 * tk)
        carry = carry_ref[...]
        a = jnp.exp(log_beta + sums[:, :, :tk] + carry)
        if masked:
            a = jnp.where(mask, a, 0.0)
        a = a.astype(BF16)
        carry = carry + sums[:, :, tk:]
        carry_ref[...] = carry
        for b in range(streams):
            v_tile = v_ref[b, pl.ds(k0, tk), :]
            for p in range(pairs):
                head = b * SB_HEADS + 2 * p
                a2 = jnp.concatenate([a[head], a[head + 1]], axis=1)
                acc_ref[b, :, pair_lanes[p]] += _dot(a2, split_heads(v_tile[:, pair_lanes[p]]))
        return (jnp.max(carry) > -EXP_UNDERFLOW).astype(I32)

    j_first = (q_first + tq - 1) // tk
    go = step(j_first, True)
    lax.while_loop(lambda st: jnp.logical_and(st[0] >= 0, st[1] > 0),
                   lambda st: (st[0] - 1, step(st[0], False)), (j_first - 1, go))
    o_ref[...] = acc_ref[...].astype(BF16)


def _attn(qb, kb, vb, *, q_pos0):
    b, t, w = qb.shape
    s_k = kb.shape[1]
    tq = min(MAX_QUERY_TILE, t)
    tk = KEY_TILE
    g = ATTN_STREAMS if b % ATTN_STREAMS == 0 else 1
    assert t % tq == 0 and s_k % tk == 0 and s_k >= q_pos0 + t
    assert tk % tq == 0 and q_pos0 % tq == 0
    r = lax.broadcasted_iota(I32, (2 * tk, 2 * tk), 0) % tk
    c = lax.broadcasted_iota(I32, (2 * tk, 2 * tk), 1)
    tri = jnp.logical_or(r > c, c >= tk).astype(BF16)
    return pl.pallas_call(
        functools.partial(_attn_kernel, tq=tq, q_pos0=q_pos0, streams=g),
        grid=(b // g, t // tq),
        in_specs=[
            pl.BlockSpec((g, tq, w), lambda i, j: (i, j, 0)),
            pl.BlockSpec((g, s_k, w), lambda i, j: (i, 0, 0), pipeline_mode=pl.Buffered(1)),
            pl.BlockSpec((g, s_k, w), lambda i, j: (i, 0, 0), pipeline_mode=pl.Buffered(1)),
            pl.BlockSpec((2 * tk, 2 * tk), lambda i, j: (0, 0)),
        ],
        out_specs=pl.BlockSpec((g, tq, w), lambda i, j: (i, j, 0)),
        out_shape=jax.ShapeDtypeStruct((b, t, w), BF16),
        scratch_shapes=[pltpu.VMEM((g, tq, w), F32), pltpu.VMEM((g * SB_HEADS, tq, tk), F32),
                        pltpu.VMEM((g * SB_HEADS // 2, 2 * tq, 2 * SB_HEAD_DIM), BF16)],
        compiler_params=_params("parallel", "parallel"),
        name="attn",
    )(qb, kb, vb, tri)


def _merge_kernel(x_ref, osb_ref, u_ref, uprev_ref, hist_ref, sg_ref, g0_ref, b0_ref, wsb_ref, wgrp_ref,
                  pscale_ref, wpool_ref, wo_ref, g1_ref, b1_ref, h_ref, ext_ref, *, tm, hist_valid):
    t = pl.program_id(1)
    h0 = _layer_norm(x_ref[0], g0_ref[...], b0_ref[...])
    u = u_ref[0]
    ext_ref[:HALO, :] = jnp.where(t == 0, hist_ref[0], uprev_ref[0])
    ext_ref[HALO:, :] = u
    rows_seen = t * tm + lax.broadcasted_iota(I32, (tm, 1), 0) + (1 + hist_valid)
    pool_out = []
    for g, window in enumerate(POOL_WINDOWS):
        lanes = slice(g * POOL_GROUP_WIDTH, (g + 1) * POOL_GROUP_WIDTH)
        total = u[:, lanes]
        for back in range(1, window):
            total = total + ext_ref[HALO - back:HALO - back + tm, lanes]
        count = jnp.minimum(rows_seen, window).astype(F32)
        pooled = total / count - u[:, lanes]
        pool_out.append(_dot(pooled.astype(BF16), wgrp_ref[g]))
    o_pool = jnp.concatenate(pool_out, axis=1) * pscale_ref[...]
    a = _dot(osb_ref[0], wsb_ref[...])
    p = _dot(o_pool.astype(BF16), wpool_ref[...])
    sg = sg_ref[0].astype(F32)
    m = sg[:, :D_MODEL] * a + sg[:, D_MODEL:] * p
    mix = _dot(m.astype(BF16), wo_ref[...])
    h1 = _layer_norm(DEEPNORM_ALPHA * h0 + mix, g1_ref[...], b1_ref[...])
    h_ref[...] = _to_tiles(h1)


def _merge(x, osb, u, hist, sg, ln0_g, ln0_b, wsb, wgrp, pscale, wpool, wo, ln1_g, ln1_b, *, hist_valid):
    b, t, d = x.shape
    tm = min(TOKEN_TILE, t)
    nt = t // tm
    row = lambda width: pl.BlockSpec((1, tm, width), lambda i, j: (i, j, 0))
    const = lambda shape: pl.BlockSpec(shape, lambda i, j: (0,) * len(shape))
    prev = pl.BlockSpec((1, HALO, POOL_WIDTH), lambda i, j: (i, jnp.maximum(j * (tm // HALO) - 1, 0), 0))
    return pl.pallas_call(
        functools.partial(_merge_kernel, tm=tm, hist_valid=hist_valid),
        grid=(b, nt),
        in_specs=[
            row(d), row(SB_WIDTH), row(POOL_WIDTH), prev,
            pl.BlockSpec((1, HALO, POOL_WIDTH), lambda i, j: (i, 0, 0)),
            row(2 * d), const((1, d)), const((1, d)), const((SB_WIDTH, d)),
            const((len(POOL_WINDOWS), POOL_GROUP_WIDTH, POOL_GROUP_WIDTH)), const((1, POOL_WIDTH)),
            const((POOL_WIDTH, d)), const((d, d)), const((1, d)), const((1, d)),
        ],
        out_specs=pl.BlockSpec((tm,) + ROW_TILE, lambda i, j: (i * nt + j, 0, 0)),
        out_shape=jax.ShapeDtypeStruct((b * t,) + ROW_TILE, F32),
        scratch_shapes=[pltpu.VMEM((HALO + tm, POOL_WIDTH), F32)],
        compiler_params=_params("parallel", "parallel"),
        name="merge",
    )(x, osb, u, u, hist, sg, ln0_g.reshape(1, d), ln0_b.reshape(1, d), wsb, wgrp, pscale.reshape(1, -1),
      wpool, wo, ln1_g.reshape(1, d), ln1_b.reshape(1, d))


def _first_max(vals, index, n):
    top = jnp.max(vals, axis=0, keepdims=True)
    first = jnp.min(jnp.where(vals == top, index, n), axis=0, keepdims=True)
    return top, first


def _route_kernel(h_ref, wr_ref, bias_ref, tri_ref, cin_ref, wsg_ref, wsu_ref, wsd_ref,
                  idx_ref, gate_ref, rank_ref, cout_ref, base_ref, *, tm):
    @pl.when(pl.program_id(0) == 0)
    def _():
        cout_ref[...] = cin_ref[...]

    h32 = _from_tiles(h_ref[...])
    h = h32.astype(BF16)
    shared_gate = _dot(h, wsg_ref[...])
    shared_hid = shared_gate * _sigmoid(shared_gate) * _dot(h, wsu_ref[...])
    base_ref[...] = DEEPNORM_ALPHA * h32 + _dot(shared_hid.astype(BF16), wsd_ref[...])
    scores = _sigmoid(lax.dot_general(wr_ref[...], h, _NT, preferred_element_type=F32))
    biased = scores + bias_ref[...]
    neg = jnp.float32(-jnp.inf)

    in_group = lax.broadcasted_iota(I32, (GROUP_SIZE, tm), 0)
    group_scores = []
    for g in range(N_EXPERT_GROUPS):
        blk = biased[g * GROUP_SIZE:(g + 1) * GROUP_SIZE, :]
        top, first = _first_max(blk, in_group, GROUP_SIZE)
        second = jnp.max(jnp.where(in_group == first, neg, blk), axis=0, keepdims=True)
        group_scores.append(top + second)
    remaining = jnp.concatenate(group_scores, axis=0)
    group_id = lax.broadcasted_iota(I32, (N_EXPERT_GROUPS, tm), 0)
    kept = jnp.zeros((N_EXPERT_GROUPS, tm), F32)
    for _ in range(TOPK_GROUPS):
        _, first = _first_max(remaining, group_id, N_EXPERT_GROUPS)
        hit = group_id == first
        kept = jnp.where(hit, 1.0, kept)
        remaining = jnp.where(hit, neg, remaining)
    remaining = jnp.concatenate(
        [jnp.where(kept[g:g + 1, :] > 0.5, biased[g * GROUP_SIZE:(g + 1) * GROUP_SIZE, :], neg)
         for g in range(N_EXPERT_GROUPS)], axis=0)

    expert_id = lax.broadcasted_iota(I32, (N_EXPERTS, tm), 0)
    chosen, picks, picked_scores = jnp.zeros((N_EXPERTS, tm), F32), [], []
    for _ in range(TOP_K):
        _, first = _first_max(remaining, expert_id, N_EXPERTS)
        hit = expert_id == first
        picks.append((first, hit))
        picked_scores.append(jnp.sum(jnp.where(hit, scores, 0.0), axis=0, keepdims=True))
        chosen = jnp.where(hit, 1.0, chosen)
        remaining = jnp.where(hit, neg, remaining)
    denom = picked_scores[0]
    for sc in picked_scores[1:]:
        denom = denom + sc
    gate_ref[...] = jnp.concatenate([sc / denom * ROUTED_SCALE for sc in picked_scores], axis=0)
    idx_ref[...] = jnp.concatenate([first for first, _ in picks], axis=0)

    slot = cout_ref[...] + _dot(chosen.astype(BF16), tri_ref[...])
    rank_ref[...] = jnp.concatenate(
        [jnp.sum(jnp.where(hit, slot, 0.0), axis=0, keepdims=True) for _, hit in picks], axis=0).astype(I32)
    cout_ref[...] += jnp.sum(chosen, axis=1, keepdims=True)


def _route(h, wr_t, bias, count_in, wsg, wsu, wsd):
    n, d = h.shape[0], D_MODEL
    tm = min(TOKEN_TILE, n)
    assert n % tm == 0
    earlier = (lax.broadcasted_iota(I32, (tm, tm), 0) < lax.broadcasted_iota(I32, (tm, tm), 1)).astype(BF16)
    const = lambda shape: pl.BlockSpec(shape, lambda i: (0,) * len(shape))
    picks = pl.BlockSpec((TOP_K, tm), lambda i: (0, i))
    return pl.pallas_call(
        functools.partial(_route_kernel, tm=tm),
        grid=(n // tm,),
        in_specs=[pl.BlockSpec((tm,) + ROW_TILE, lambda i: (i, 0, 0)), const((N_EXPERTS, d)),
                  const((N_EXPERTS, 1)), const((tm, tm)), const((N_EXPERTS, 1)),
                  const((d, EXPERT_DIM)), const((d, EXPERT_DIM)), const((EXPERT_DIM, d))],
        out_specs=[picks, picks, picks, const((N_EXPERTS, 1)), pl.BlockSpec((tm, d), lambda i: (i, 0))],
        out_shape=[jax.ShapeDtypeStruct((TOP_K, n), I32), jax.ShapeDtypeStruct((TOP_K, n), F32),
                   jax.ShapeDtypeStruct((TOP_K, n), I32), jax.ShapeDtypeStruct((N_EXPERTS, 1), F32),
                   jax.ShapeDtypeStruct((n, d), F32)],
        compiler_params=_params("arbitrary"),
        name="route",
    )(h, wr_t, bias, earlier, count_in, wsg, wsu, wsd)


def _slot_kernel(idx_ref, rank_ref, start_ref, dest_ref, *, tm):
    expert_id = lax.broadcasted_iota(I32, (N_EXPERTS, tm), 0)
    start = start_ref[...]
    rows = [jnp.sum(jnp.where(expert_id == idx_ref[j:j + 1, :], start, 0.0), axis=0, keepdims=True)
            for j in range(TOP_K)]
    dest_ref[...] = jnp.concatenate(rows, axis=0).astype(I32) + rank_ref[...]


def _slots(idx, rank, start):
    n = idx.shape[1]
    tm = min(TOKEN_TILE, n)
    assert n % tm == 0
    picks = pl.BlockSpec((TOP_K, tm), lambda i: (0, i))
    return pl.pallas_call(
        functools.partial(_slot_kernel, tm=tm),
        grid=(n // tm,),
        in_specs=[picks, picks, pl.BlockSpec((N_EXPERTS, 1), lambda i: (0, 0))],
        out_specs=picks,
        out_shape=jax.ShapeDtypeStruct((TOP_K, n), I32),
        compiler_params=_params("parallel"),
        name="slots",
    )(idx, rank, start)


def _start_row_copies(tm, copy_of):
    def group(g, carry):
        tok0 = pl.multiple_of(g * SUBLANES, SUBLANES)
        for r in range(SUBLANES):
            for j in range(TOP_K):
                copy_of(tok0 + r, j, tok0 * TOP_K + (r * TOP_K + j)).start(priority=(r + j) % 2)
        return carry

    lax.fori_loop(0, tm // SUBLANES, group, 0)


def _dispatch_kernel(dest_ref, h_ref, *rest, tm):
    xs_ref, sem = rest[-2], rest[-1]
    _start_row_copies(
        tm, lambda tok, j, pair: pltpu.make_async_copy(h_ref.at[tok], xs_ref.at[dest_ref[pair]], sem))
    for _ in range(TOP_K):
        pltpu.make_async_copy(h_ref, xs_ref.at[pl.ds(0, tm)], sem).wait()


def _dispatch(dest, h, xs, cap):
    n = h.shape[0]
    tm = min(SCATTER_TILE, n)
    assert n % tm == 0
    in_specs = [pl.BlockSpec((tm * TOP_K,), lambda i: (i,), memory_space=pltpu.SMEM),
                pl.BlockSpec((tm,) + ROW_TILE, lambda i: (i, 0, 0))]
    args = [dest, h]
    aliases = {}
    if xs is not None:
        in_specs.append(pl.BlockSpec(memory_space=pl.ANY))
        args.append(xs)
        aliases = {2: 0}
    return pl.pallas_call(
        functools.partial(_dispatch_kernel, tm=tm),
        grid=(n // tm,),
        in_specs=in_specs,
        out_specs=pl.BlockSpec(memory_space=pl.ANY),
        out_shape=jax.ShapeDtypeStruct((cap,) + ROW_TILE, F32),
        scratch_shapes=[pltpu.SemaphoreType.DMA],
        input_output_aliases=aliases,
        compiler_params=pltpu.CompilerParams(dimension_semantics=("arbitrary",), vmem_limit_bytes=VMEM_LIMIT,
                                             has_side_effects=True),
        name="dispatch",
    )(*args)


def _expert_kernel(first_ref, nblk_ref, used_ref, wg_ref, wu_ref, wd_ref, xs_ref, ys_ref,
                   wg_b, wu_b, wd_b, x_buf, y_buf, x_sem, y_sem):
    e = pl.program_id(0)
    blk, nx, ny, ahead = EXPERT_BLOCK, EXPERT_X_SLOTS, EXPERT_Y_SLOTS, EXPERT_X_AHEAD
    first, n, used = first_ref[e], nblk_ref[e], used_ref[0]

    def x_copy(g):
        slot = g % nx
        return pltpu.make_async_copy(xs_ref.at[pl.ds(pl.multiple_of(g * blk, blk), blk)], x_buf.at[slot],
                                     x_sem.at[slot])

    def y_copy(g):
        slot = g % ny
        return pltpu.make_async_copy(y_buf.at[slot], ys_ref.at[pl.ds(pl.multiple_of(g * blk, blk), blk)],
                                     y_sem.at[slot])

    @pl.when(e == 0)
    def _():
        for g in range(ahead):
            @pl.when(g < used)
            def _():
                x_copy(g).start()

    def compute(blocks):
        for g in blocks:
            @pl.when(g + ahead < used)
            def _():
                x_copy(g + ahead).start()
        for g in blocks:
            x_copy(g).wait()
        x = jnp.concatenate([_from_tiles(x_buf[g % nx]) for g in blocks], axis=0).astype(BF16)
        gate = _dot(x, wg_b[...])
        hid = gate * _sigmoid(gate) * _dot(x, wu_b[...])
        y = _dot(hid.astype(BF16), wd_b[...])
        for k, g in enumerate(blocks):
            @pl.when(g >= ny)
            def _():
                y_copy(g - ny).wait()

            y_buf[g % ny] = _to_tiles(y[k * blk:(k + 1) * blk])
            y_copy(g).start()

    @pl.when(n > 0)
    def _():
        wg_b[...] = wg_ref[0].astype(BF16)
        wu_b[...] = wu_ref[0].astype(BF16)
        wd_b[...] = wd_ref[0].astype(BF16)

        def pair(p, carry):
            compute([first + 2 * p, first + 2 * p + 1])
            return carry

        lax.fori_loop(0, n // 2, pair, 0)

        @pl.when(n % 2 == 1)
        def _():
            compute([first + n - 1])

    @pl.when(e == pl.num_programs(0) - 1)
    def _():
        for back in range(ny, 0, -1):
            @pl.when(used >= back)
            def _():
                y_copy(used - back).wait()


def _experts(first, nblk, used, xs, w_gate, w_up, w_down):
    cap = xs.shape[0]
    blk = EXPERT_BLOCK
    gate_up = pl.BlockSpec((1, D_MODEL, EXPERT_DIM), lambda e, *_: (e, 0, 0))
    return pl.pallas_call(
        _expert_kernel,
        grid_spec=pltpu.PrefetchScalarGridSpec(
            num_scalar_prefetch=3,
            grid=(N_EXPERTS,),
            in_specs=[gate_up, gate_up, pl.BlockSpec((1, EXPERT_DIM, D_MODEL), lambda e, *_: (e, 0, 0)),
                      pl.BlockSpec(memory_space=pl.ANY)],
            out_specs=pl.BlockSpec(memory_space=pl.ANY),
            scratch_shapes=[pltpu.VMEM((D_MODEL, EXPERT_DIM), BF16), pltpu.VMEM((D_MODEL, EXPERT_DIM), BF16),
                            pltpu.VMEM((EXPERT_DIM, D_MODEL), BF16),
                            pltpu.VMEM((EXPERT_X_SLOTS, blk) + ROW_TILE, F32),
                            pltpu.VMEM((EXPERT_Y_SLOTS, blk) + ROW_TILE, F32),
                            pltpu.SemaphoreType.DMA((EXPERT_X_SLOTS,)), pltpu.SemaphoreType.DMA((EXPERT_Y_SLOTS,))],
        ),
        out_shape=jax.ShapeDtypeStruct((cap,) + ROW_TILE, F32),
        compiler_params=_params("arbitrary"),
        name="experts",
    )(first, nblk, used, w_gate, w_up, w_down, xs)


def _combine_kernel(dest0_ref, dest1_ref, base_ref, gate_ref, ys_ref, g_ref, b_ref, o_ref, rows_ref, sem, *, tm):
    i = pl.program_id(0)
    slot = i % 2
    has_next = i + 1 < pl.num_programs(0)

    def gather(dest_ref, into):
        _start_row_copies(tm, lambda tok, j, pair: pltpu.make_async_copy(
            ys_ref.at[dest_ref[pair]], rows_ref.at[into, j, tok], sem.at[into]))

    @pl.when(i == 0)
    def _():
        gather(dest0_ref, 0)

    for j in range(TOP_K):
        pltpu.make_async_copy(ys_ref.at[pl.ds(0, tm)], rows_ref.at[slot, j], sem.at[slot]).wait()

    def finish(rows):
        weights = gate_ref[rows, :]
        total = base_ref[rows, :]
        for j in range(TOP_K):
            total = total + weights[:, j:j + 1] * _from_tiles(rows_ref[slot, j, rows])
        o_ref[rows, :] = _layer_norm(total, g_ref[...], b_ref[...])

    chunk = min(COMBINE_CHUNK, tm)
    chunks = [slice(c * chunk, (c + 1) * chunk) for c in range(tm // chunk)]

    @pl.when(has_next)
    def _():
        for rows in chunks:
            for tok in range(rows.start, rows.stop):
                for j in range(TOP_K):
                    pltpu.make_async_copy(ys_ref.at[dest1_ref[tok * TOP_K + j]], rows_ref.at[1 - slot, j, tok],
                                          sem.at[1 - slot]).start(priority=(tok + j) % 2)
            finish(rows)

    @pl.when(jnp.logical_not(has_next))
    def _():
        for rows in chunks:
            finish(rows)


def _combine(dest, base, gate_t, ys, ln_g, ln_b):
    n, d = base.shape
    tm = min(GATHER_TILE, n)
    assert n % tm == 0
    const = lambda shape: pl.BlockSpec(shape, lambda i: (0,) * len(shape))
    steps = n // tm
    return pl.pallas_call(
        functools.partial(_combine_kernel, tm=tm),
        grid=(steps,),
        in_specs=[pl.BlockSpec((tm * TOP_K,), lambda i: (0,), memory_space=pltpu.SMEM),
                  pl.BlockSpec((tm * TOP_K,), lambda i: (jnp.minimum(i + 1, steps - 1),), memory_space=pltpu.SMEM),
                  pl.BlockSpec((tm, d), lambda i: (i, 0)),
                  pl.BlockSpec((tm, TOP_K), lambda i: (i, 0)),
                  pl.BlockSpec(memory_space=pl.ANY),
                  const((1, d)), const((1, d))],
        out_specs=pl.BlockSpec((tm, d), lambda i: (i, 0)),
        out_shape=jax.ShapeDtypeStruct((n, d), F32),
        scratch_shapes=[pltpu.VMEM((2, TOP_K, tm) + ROW_TILE, F32), pltpu.SemaphoreType.DMA((2,))],
        compiler_params=_params("arbitrary"),
        name="combine",
    )(dest, dest, base, gate_t, ys, ln_g.reshape(1, d), ln_b.reshape(1, d))


def _key_major(cache):
    b, h, s, hd = cache.shape
    return cache.transpose(0, 2, 1, 3).reshape(b, s, h * hd).astype(BF16)


def kernel(x_prompt, x_sample, cache_k, cache_v, state_pool, ln0_g, ln0_b, w_in, w_sb_up, w_pool_grp, pool_scale,
           w_pool_up, w_o, ln1_g, ln1_b, w_router, router_bias, w_exp_gate, w_exp_up, w_exp_down, w_sh_gate,
           w_sh_up, w_sh_down, ln2_g, ln2_b):
    assert w_in.shape[0] == DEPTH
    b_p, t_p, d = x_prompt.shape
    b_s, t_s, _ = x_sample.shape
    past = cache_k.shape[3]
    n_p, n_s = b_p * t_p, b_s * t_s
    layer = 0
    w_in_b = w_in[layer].astype(BF16)
    mix_w = (w_sb_up[layer].astype(BF16), w_pool_grp[layer].astype(BF16), pool_scale[layer],
             w_pool_up[layer].astype(BF16), w_o[layer].astype(BF16), ln1_g[layer], ln1_b[layer])

    qb, kb, vb, k_p, v_p, u_p, sg_p = _proj(x_prompt, ln0_g, ln0_b, w_in_b)
    osb_p = _attn(qb, kb, vb, q_pos0=0)
    hist_p = jnp.zeros((b_p, HALO, POOL_WIDTH), F32)
    h_p = _merge(x_prompt, osb_p, u_p, hist_p, sg_p, ln0_g, ln0_b, *mix_w, hist_valid=0)

    qs, ks, vs, k_s, v_s, u_s, sg_s = _proj(x_sample, ln0_g, ln0_b, w_in_b)
    key_rows = -(-(past + t_s) // KEY_TILE) * KEY_TILE
    pad = ((0, 0), (0, key_rows - past - t_s), (0, 0))
    k_all = jnp.pad(jnp.concatenate([_key_major(cache_k[layer]), ks], axis=1), pad)
    v_all = jnp.pad(jnp.concatenate([_key_major(cache_v[layer]), vs], axis=1), pad)
    osb_s = _attn(qs, k_all, v_all, q_pos0=past)
    hist_s = jnp.concatenate([jnp.zeros((b_s, 1, POOL_WIDTH), F32), state_pool[layer]], axis=1)
    h_s = _merge(x_sample, osb_s, u_s, hist_s, sg_s, ln0_g, ln0_b, *mix_w, hist_valid=POOL_STATE)

    wr_t = w_router[layer].T.astype(BF16)
    bias = router_bias[layer].astype(F32).reshape(N_EXPERTS, 1)
    shared_w = (w_sh_gate[layer].astype(BF16), w_sh_up[layer].astype(BF16), w_sh_down[layer].astype(BF16))
    idx_p, gate_p, rank_p, count, base_p = _route(h_p, wr_t, bias, jnp.zeros((N_EXPERTS, 1), F32), *shared_w)
    idx_s, gate_s, rank_s, count, base_s = _route(h_s, wr_t, bias, count, *shared_w)

    blk = EXPERT_BLOCK
    n_blocks = ((n_p + n_s) * TOP_K + N_EXPERTS * (blk - 1) + blk - 1) // blk
    cap = n_blocks * blk
    assert cap < 2 ** 24
    nblk = (count[:, 0].astype(I32) + blk - 1) // blk
    first = jnp.cumsum(nblk) - nblk
    used = jnp.sum(nblk).reshape(1)
    start = (first * blk).astype(F32).reshape(N_EXPERTS, 1)
    dest_p = _slots(idx_p, rank_p, start).T.reshape(-1)
    dest_s = _slots(idx_s, rank_s, start).T.reshape(-1)

    xs = _dispatch(dest_p, h_p, None, cap)
    xs = _dispatch(dest_s, h_s, xs, cap)
    ys = _experts(first, nblk, used, xs, w_exp_gate[layer], w_exp_up[layer], w_exp_down[layer])

    y_p = _combine(dest_p, base_p, gate_p.T, ys, ln2_g[layer], ln2_b[layer])
    y_s = _combine(dest_s, base_s, gate_s.T, ys, ln2_g[layer], ln2_b[layer])

    new_pool_p = u_p[:, t_p - POOL_STATE:][None]
    new_pool_s = jnp.concatenate([state_pool[layer], u_s], axis=1)[:, -POOL_STATE:][None]
    return (y_p.reshape(b_p, t_p, d), y_s.reshape(b_s, t_s, d), k_p, v_p, new_pool_p, k_s, v_s, new_pool_s)
```
